```python
import functools
import jax
import jax.numpy as jnp
from jax import lax
import numpy as np

D_MODEL = 1024
BATCH = 16
SEQ = 2048
DEPTH = 1
DEC_BATCH = 32
DEC_SEQ = 8
PAST_LEN = 16384
PAGE_SIZE = 128

HEAD_DIM = 64
N_HEADS_A = 8
WIDTH_A = N_HEADS_A * HEAD_DIM
MOBA_BLOCK = 256
MOBA_TOPK = 3
Q_CHUNK = 128
N_HEADS_B = 8
WIDTH_B = N_HEADS_B * HEAD_DIM
D_DECAY_LORA = 64
D_AAA_LORA = 64
D_GATE_LORA = 128
RWKV_PROJ = 3 * WIDTH_B + D_DECAY_LORA + D_AAA_LORA + D_GATE_LORA
PROJ_W = 3 * WIDTH_A + RWKV_PROJ + 2 * D_MODEL
D_FF = 2816
CONV_W = 3
RMS_EPS = 1e-6
GN_EPS = 64e-5
SPLIT_IN = (WIDTH_A, 2 * WIDTH_A, 3 * WIDTH_A, 3 * WIDTH_A + RWKV_PROJ)
SPLIT_B = (WIDTH_B, 2 * WIDTH_B, 3 * WIDTH_B, 3 * WIDTH_B + D_DECAY_LORA, 3 * WIDTH_B + D_DECAY_LORA + D_AAA_LORA)

kernel_name = 'hybrid_moba_rwkv7_convffn_step'


def rmsnorm(x, g):
    xf = x.astype(jnp.float32)
    y = xf * lax.rsqrt(jnp.mean(xf * xf, axis=-1, keepdims=True) + RMS_EPS)
    return (y * g.astype(jnp.float32)).astype(x.dtype)


def moba_prompt(q, k, v):
    B, S, H, Dh = q.shape
    f32 = jnp.float32
    scale = Dh ** -0.5
    nb = -(-S // MOBA_BLOCK)
    pad = ((0, 0), (0, nb * MOBA_BLOCK - S), (0, 0), (0, 0))
    kb = jnp.pad(k, pad).reshape(B, nb, MOBA_BLOCK, H, Dh)
    vb = jnp.pad(v, pad).reshape(B, nb, MOBA_BLOCK, H, Dh)
    n_sel = min(MOBA_TOPK, nb - 1)
    n_chunk = S // Q_CHUNK
    hix = jnp.arange(H)[:, None, None]
    if n_sel > 0:
        kmean = jnp.mean(kb.astype(f32), axis=2)
        gate = jnp.einsum('bshd,bjhd->bhsj', q.astype(f32), kmean)
        past = jnp.arange(nb)[None, :] < (jnp.arange(S) // MOBA_BLOCK)[:, None]
        _, sel = lax.top_k(jnp.where(past, gate, -jnp.inf), n_sel)

    def chunk_fn(bc):
        b = bc // n_chunk
        q0 = (bc % n_chunk) * Q_CHUNK
        blk = q0 // MOBA_BLOCK
        qc = lax.dynamic_slice_in_dim(q[b], q0, Q_CHUNK, axis=0)
        k_own = kb[b, blk]
        v_own = vb[b, blk]
        qpos = q0 % MOBA_BLOCK + jnp.arange(Q_CHUNK)
        causal = jnp.arange(MOBA_BLOCK)[None, :] <= qpos[:, None]
        l_own = jnp.einsum('qhd,khd->hqk', qc, k_own).astype(f32) * scale
        l_own = jnp.where(causal[None], l_own, -jnp.inf)
        if n_sel == 0:
            p = jax.nn.softmax(l_own, axis=-1)
            o = jnp.einsum('hqk,khd->qhd', p, v_own.astype(f32))
        else:
            sel_c = lax.dynamic_slice_in_dim(sel[b], q0, Q_CHUNK, axis=1)
            k_g = kb[b][sel_c, :, hix]
            v_g = vb[b][sel_c, :, hix]
            l_sel = jnp.einsum('qhd,hqjkd->hqjk', qc, k_g).astype(f32) * scale
            valid = jnp.arange(n_sel) < blk
            l_sel = jnp.where(valid[None, None, :, None], l_sel, -jnp.inf)
            n_g = n_sel * MOBA_BLOCK
            p = jax.nn.softmax(jnp.concatenate([l_sel.reshape(H, Q_CHUNK, n_g), l_own], axis=-1), axis=-1)
            o = (jnp.einsum('hqjk,hqjkd->qhd', p[..., :n_g].reshape(H, Q_CHUNK, n_sel, MOBA_BLOCK), v_g.astype(f32))
                 + jnp.einsum('hqk,khd->qhd', p[..., n_g:], v_own.astype(f32)))
        return o.astype(q.dtype)

    out = lax.map(chunk_fn, jnp.arange(B * n_chunk))
    return out.reshape(B, S, H, Dh)


def moba_sample(q, k, v, cache_k, cache_v, page_table):
    DB, T, H, Dh = q.shape
    f32 = jnp.float32
    scale = Dh ** -0.5
    past_len = page_table.shape[1] * PAGE_SIZE
    ppb = MOBA_BLOCK // PAGE_SIZE
    c = past_len // MOBA_BLOCK
    n_sel = min(MOBA_TOPK, c)
    own_start = c * MOBA_BLOCK
    n_oc = past_len - own_start
    qf = q.astype(f32)
    parts = []
    if n_sel > 0:
        k_full = cache_k[page_table[:, :c * ppb]].astype(f32).reshape(DB, c, MOBA_BLOCK, H, Dh)
        kmean = jnp.mean(k_full, axis=2)
        gate = jnp.einsum('bthd,bjhd->bhtj', qf, kmean)
        _, sel = lax.top_k(gate, n_sel)
        logical = sel[..., None] * ppb + jnp.arange(ppb)
        phys = page_table[jnp.arange(DB)[:, None, None, None, None], logical]
        hix = jnp.arange(H)[None, :, None, None, None]
        k_g = cache_k[phys, :, hix].reshape(DB, H, T, n_sel * MOBA_BLOCK, Dh)
        v_g = cache_v[phys, :, hix].reshape(DB, H, T, n_sel * MOBA_BLOCK, Dh)
        parts.append((jnp.einsum('bthd,bhtkd->bhtk', qf, k_g.astype(f32)), v_g, 'bhtk,bhtkd->bthd'))
    if n_oc > 0:
        pages = page_table[:, own_start // PAGE_SIZE:]
        k_oc = cache_k[pages].reshape(DB, n_oc, H, Dh)
        v_oc = cache_v[pages].reshape(DB, n_oc, H, Dh)
        parts.append((jnp.einsum('bthd,bkhd->bhtk', qf, k_oc.astype(f32)), v_oc, 'bhtk,bkhd->bthd'))
    causal = jnp.tril(jnp.ones((T, T), dtype=bool))
    l_new = jnp.where(causal, jnp.einsum('bthd,bshd->bhts', qf, k.astype(f32)), -jnp.inf)
    parts.append((l_new, v, 'bhts,bshd->bthd'))
    probs = jax.nn.softmax(jnp.concatenate([lp for lp, _, _ in parts], axis=-1) * scale, axis=-1)
    cuts = [int(s) for s in np.cumsum([lp.shape[-1] for lp, _, _ in parts])[:-1]]
    out = jnp.zeros((DB, T, H, Dh), f32)
    for pr, (_, vals, spec) in zip(jnp.split(probs, cuts, axis=-1), parts):
        out = out + jnp.einsum(spec, pr, vals.astype(f32))
    return out.astype(q.dtype)


def rwkv7_branch(pb, prev_row, s0, mu, w0, w2, a0, a2, g2, k_k, k_a, r_k, lnx_g, lnx_b):
    B, T, _ = pb.shape
    f32 = jnp.float32
    H, N = N_HEADS_B, HEAD_DIM
    prev = jnp.concatenate([prev_row[:, None, :].astype(pb.dtype), pb[:, :-1]], axis=1)
    xm = pb + (prev - pb) * mu
    r, k, v, wd, ad, gd = jnp.split(xm, SPLIT_B, axis=-1)
    logw = -jax.nn.softplus(-(w0 + jnp.tanh(wd) @ w2).astype(f32)) - 0.5
    decay = jnp.exp(-jnp.exp(logw))
    a = jax.nn.sigmoid((a0 + ad @ a2).astype(f32))
    g = jax.nn.sigmoid(gd) @ g2
    hd = lambda t: t.astype(f32).reshape(B, T, H, N)
    r, k, v, decay, a = hd(r), hd(k), hd(v), hd(decay), hd(a)
    kk = k * k_k.astype(f32).reshape(H, N)
    kk = kk / jnp.maximum(jnp.linalg.norm(kk, axis=-1, keepdims=True), 1e-12)
    k = k * (1.0 + (a - 1.0) * k_a.astype(f32).reshape(H, N))

    def step(S, inp):
        r_t, w_t, k_t, v_t, a_t, b_t = inp
        sa = jnp.einsum('bhij,bhj->bhi', S, a_t)
        S = S * w_t[:, :, None, :] + sa[..., None] * b_t[:, :, None, :] + v_t[..., None] * k_t[:, :, None, :]
        return S, jnp.einsum('bhij,bhj->bhi', S, r_t)

    tm = lambda t: jnp.swapaxes(t, 0, 1)
    s_fin, y = lax.scan(step, s0.astype(f32), (tm(r), tm(decay), tm(k), tm(v), tm(-kk), tm(kk * a)))
    y = tm(y)
    mean = jnp.mean(y, axis=-1, keepdims=True)
    var = jnp.mean(jnp.square(y - mean), axis=-1, keepdims=True)
    yn = (y - mean) * lax.rsqrt(var + GN_EPS) * lnx_g.astype(f32).reshape(H, N) + lnx_b.astype(f32).reshape(H, N)
    bonus = jnp.sum(r * k * r_k.astype(f32), axis=-1, keepdims=True) * v
    out = (yn + bonus).reshape(B, T, WIDTH_B).astype(pb.dtype) * g
    return out, s_fin, pb[:, -1]


def trunk_layer(x, attend, shift_prev, wkv0, conv_prev, ln1_g, w_in, w_br_a, mu, w0, w2, a0, a2, g2,
                k_k, k_a, r_k, lnx_g, lnx_b, w_br_b, w_o, ln2_g, w_up, conv_w, conv_b, w_down):
    B, T, _ = x.shape
    xn = rmsnorm(x, ln1_g)
    qa, ka, va, pb, gates = jnp.split(xn @ w_in, SPLIT_IN, axis=-1)
    heads = lambda t: t.reshape(B, T, N_HEADS_A, HEAD_DIM)
    qa, ka, va = heads(qa), heads(ka), heads(va)
    y_a = attend(qa, ka, va).reshape(B, T, WIDTH_A) @ w_br_a
    y_b, wkv_new, shift_new = rwkv7_branch(pb, shift_prev, wkv0, mu, w0, w2, a0, a2, g2, k_k, k_a, r_k, lnx_g, lnx_b)
    y_b = y_b @ w_br_b
    g_a, g_b = jnp.split(jax.nn.sigmoid(gates), 2, axis=-1)
    h = x + (g_a * y_a + g_b * y_b) @ w_o
    u = rmsnorm(h, ln2_g) @ w_up
    uc = jnp.concatenate([conv_prev.astype(u.dtype), u], axis=1)
    cv = conv_b
    for i in range(CONV_W):
        cv = cv + uc[:, i:i + T] * conv_w[i]
    val, gt = jnp.split(cv, 2, axis=-1)
    out = h + (jax.nn.silu(gt) * val) @ w_down
    return out, (ka, va, wkv_new, shift_new, uc[:, -(CONV_W - 1):])


def setup_inputs(seed: int = 0) -> dict:
    key = jax.random.key(seed)
    ks = jax.random.split(key, 32)
    f32 = jnp.float32
    nrm = lambda k, shape, s: jax.random.normal(k, shape, f32) * s
    L = DEPTH
    n_pages = PAST_LEN // PAGE_SIZE
    n_used = DEC_BATCH * n_pages
    n_pool = n_used + max(1, n_used // 4)
    page_table = jax.random.permutation(ks[4], n_pool)[:n_used].reshape(DEC_BATCH, n_pages).astype(jnp.int32)
    return {
        'x_prompt': nrm(ks[0], (BATCH, SEQ, D_MODEL), 1.0),
        'x_sample': nrm(ks[1], (DEC_BATCH, DEC_SEQ, D_MODEL), 1.0),
        'cache_k': nrm(ks[2], (L, n_pool, PAGE_SIZE, N_HEADS_A, HEAD_DIM), 1.0),
        'cache_v': nrm(ks[3], (L, n_pool, PAGE_SIZE, N_HEADS_A, HEAD_DIM), 1.0),
        'page_table': page_table,
        'state_wkv': nrm(ks[5], (L, DEC_BATCH, N_HEADS_B, HEAD_DIM, HEAD_DIM), 0.3),
        'state_tshift': nrm(ks[6], (L, DEC_BATCH, RWKV_PROJ), 1.0),
        'state_conv': nrm(ks[7], (L, DEC_BATCH, CONV_W - 1, 2 * D_FF), 1.0),
        'ln1_g': 1.0 + nrm(ks[8], (L, D_MODEL), 0.02),
        'w_in': nrm(ks[9], (L, D_MODEL, PROJ_W), D_MODEL ** -0.5),
        'w_br_a': nrm(ks[10], (L, WIDTH_A, D_MODEL), WIDTH_A ** -0.5),
        'mu': jax.random.uniform(ks[11], (L, RWKV_PROJ), f32),
        'w0': jax.random.uniform(ks[12], (L, WIDTH_B), f32, -5.0, 1.0),
        'w2': nrm(ks[13], (L, D_DECAY_LORA, WIDTH_B), 0.1),
        'a0': nrm(ks[14], (L, WIDTH_B), 0.1),
        'a2': nrm(ks[15], (L, D_AAA_LORA, WIDTH_B), D_AAA_LORA ** -0.5),
        'g2': nrm(ks[16], (L, D_GATE_LORA, WIDTH_B), D_GATE_LORA ** -0.5),
        'k_k': 0.85 + nrm(ks[17], (L, WIDTH_B), 0.05),
        'k_a': 1.0 + nrm(ks[18], (L, WIDTH_B), 0.05),
        'r_k': nrm(ks[19], (L, N_HEADS_B, HEAD_DIM), 0.1),
        'lnx_g': 1.0 + nrm(ks[20], (L, WIDTH_B), 0.02),
        'lnx_b': nrm(ks[21], (L, WIDTH_B), 0.01),
        'w_br_b': nrm(ks[22], (L, WIDTH_B, D_MODEL), WIDTH_B ** -0.5),
        'w_o': nrm(ks[23], (L, D_MODEL, D_MODEL), D_MODEL ** -0.5),
        'ln2_g': 1.0 + nrm(ks[24], (L, D_MODEL), 0.02),
        'w_up': nrm(ks[25], (L, D_MODEL, 2 * D_FF), D_MODEL ** -0.5),
        'conv_w': nrm(ks[26], (L, CONV_W, 2 * D_FF), CONV_W ** -0.5),
        'conv_b': nrm(ks[27], (L, 2 * D_FF), 0.01),
        'w_down': nrm(ks[28], (L, D_FF, D_MODEL), D_FF ** -0.5),
        'lnf_g': 1.0 + nrm(ks[29], (D_MODEL,), 0.02),
    }


def reference(x_prompt, x_sample, cache_k, cache_v, page_table, state_wkv, state_tshift, state_conv,
              ln1_g, w_in, w_br_a, mu, w0, w2, a0, a2, g2, k_k, k_a, r_k, lnx_g, lnx_b, w_br_b, w_o,
              ln2_g, w_up, conv_w, conv_b, w_down, lnf_g):
    bp = x_prompt.shape[0]
    hp, hs = x_prompt, x_sample
    new_p, new_s = [], []
    for l in range(DEPTH):
        lw = (ln1_g[l], w_in[l], w_br_a[l], mu[l], w0[l], w2[l], a0[l], a2[l], g2[l], k_k[l], k_a[l],
              r_k[l], lnx_g[l], lnx_b[l], w_br_b[l], w_o[l], ln2_g[l], w_up[l], conv_w[l], conv_b[l], w_down[l])
        zero_shift = jnp.zeros((bp, RWKV_PROJ), x_prompt.dtype)
        zero_wkv = jnp.zeros((bp, N_HEADS_B, HEAD_DIM, HEAD_DIM), jnp.float32)
        zero_conv = jnp.zeros((bp, CONV_W - 1, 2 * D_FF), x_prompt.dtype)
        hp, st_p = trunk_layer(hp, moba_prompt, zero_shift, zero_wkv, zero_conv, *lw)
        attend_s = functools.partial(moba_sample, cache_k=cache_k[l], cache_v=cache_v[l], page_table=page_table)
        hs, st_s = trunk_layer(hs, attend_s, state_tshift[l], state_wkv[l], state_conv[l], *lw)
        new_p.append(st_p)
        new_s.append(st_s)
    y_prompt = rmsnorm(hp, lnf_g)
    y_sample = rmsnorm(hs, lnf_g)
    k_prompt, v_prompt, wkv_prompt, tshift_prompt, conv_prompt = [jnp.stack(z) for z in zip(*new_p)]
    k_sample, v_sample, wkv_sample, tshift_sample, conv_sample = [jnp.stack(z) for z in zip(*new_s)]
    return (y_prompt, y_sample, k_prompt, v_prompt, wkv_prompt, tshift_prompt, conv_prompt,
            k_sample, v_sample, wkv_sample, tshift_sample, conv_sample)
```

```python
import functools

import jax
import jax.numpy as jnp
from jax import lax
from jax.experimental import pallas as pl
from jax.experimental.pallas import tpu as pltpu

F32 = jnp.float32
BF16 = jnp.bfloat16
HIGHEST = lax.Precision.HIGHEST
NT_DIMS = (((1,), (1,)), ((), ()))

HEAD_DIM = 64
N_HEADS = 8
WIDTH = N_HEADS * HEAD_DIM
MOBA_BLOCK = 256
MOBA_TOPK = 3
PAGE_SIZE = 128
PAGES_PER_BLOCK = MOBA_BLOCK // PAGE_SIZE
D_DECAY_LORA = 64
D_AAA_LORA = 64
D_GATE_LORA = 128
RWKV_PROJ = 3 * WIDTH + D_DECAY_LORA + D_AAA_LORA + D_GATE_LORA
CONV_W = 3
RMS_EPS = 1e-6
GN_EPS = 64e-5
NEG = -1e30
ATTN_SCALE = HEAD_DIM ** -0.5

LANES = 128
SUBLANES = 8
VMEM_LIMIT = 56 * 1024 * 1024


def _cparams(sem):
    return pltpu.CompilerParams(dimension_semantics=sem, vmem_limit_bytes=VMEM_LIMIT)


def _const_spec(shape):
    nd = len(shape)
    return pl.BlockSpec(shape, lambda *_: (0,) * nd)


def _sigmoid(x):
    return 1.0 / (1.0 + jnp.exp(-x))


def _rms(x, g):
    return x * lax.rsqrt(jnp.mean(x * x, axis=-1, keepdims=True) + RMS_EPS) * g


def _head_sum(x, bo):
    outs = []
    for p in range(x.shape[1] // LANES):
        xs = x[:, p * LANES:(p + 1) * LANES]
        hi = xs.astype(BF16)
        r1 = xs - hi.astype(F32)
        mid = r1.astype(BF16)
        lo = (r1 - mid.astype(F32)).astype(BF16)
        outs.append(jnp.dot(hi, bo, preferred_element_type=F32)
                    + jnp.dot(mid, bo, preferred_element_type=F32)
                    + jnp.dot(lo, bo, preferred_element_type=F32))
    return jnp.concatenate(outs, axis=1)


def _inproj_kernel(x_ref, g_ref, wq_ref, wkvt_ref, wpb_ref, wg_ref,
                   q_ref, kt_ref, vt_ref, pb_ref, sg_ref):
    xb = _rms(x_ref[...], g_ref[...]).astype(BF16)
    q_ref[...] = jnp.dot(xb, wq_ref[...], preferred_element_type=F32)
    kt_ref[...] = lax.dot_general(wkvt_ref[0:WIDTH, :], xb, NT_DIMS, preferred_element_type=F32)
    vt_ref[...] = lax.dot_general(wkvt_ref[WIDTH:2 * WIDTH, :], xb, NT_DIMS, preferred_element_type=F32)
    pb_ref[...] = jnp.dot(xb, wpb_ref[...], preferred_element_type=F32)
    sg_ref[...] = _sigmoid(jnp.dot(xb, wg_ref[...], preferred_element_type=F32))


def _inproj(x, ln_g, wq, wkvt, wpb, wg, tm):
    B, T, D = x.shape
    row = lambda w: pl.BlockSpec((None, tm, w), lambda b, t: (b, t, 0))
    col = pl.BlockSpec((None, WIDTH, tm), lambda b, t: (b, 0, t))
    return pl.pallas_call(
        _inproj_kernel,
        grid=(B, T // tm),
        in_specs=[row(D), _const_spec((1, D)), _const_spec(wq.shape), _const_spec(wkvt.shape),
                  _const_spec(wpb.shape), _const_spec(wg.shape)],
        out_specs=[row(WIDTH), col, col, row(RWKV_PROJ), row(2 * D)],
        out_shape=[jax.ShapeDtypeStruct((B, T, WIDTH), F32),
                   jax.ShapeDtypeStruct((B, WIDTH, T), F32),
                   jax.ShapeDtypeStruct((B, WIDTH, T), F32),
                   jax.ShapeDtypeStruct((B, T, RWKV_PROJ), F32),
                   jax.ShapeDtypeStruct((B, T, 2 * D), F32)],
        compiler_params=_cparams(("parallel", "parallel")),
        name="inproj",
    )(x, ln_g, wq, wkvt, wpb, wg)


def _moba_prompt_kernel(q_ref, kt_ref, vt_ref, o_ref, *, nblk):
    S = q_ref.shape[0]
    kt = kt_ref[...]
    vtb = vt_ref[...].astype(BF16)
    lane = lax.broadcasted_iota(jnp.int32, (1, LANES), 1)
    rowd = lax.broadcasted_iota(jnp.int32, (LANES, 1), 0)
    keyblk = lax.broadcasted_iota(jnp.int32, (1, S), 1) // MOBA_BLOCK
    km = jnp.zeros((LANES, LANES), F32)
    for j in range(nblk):
        colj = jnp.sum(kt[:, j * MOBA_BLOCK:(j + 1) * MOBA_BLOCK], axis=1, keepdims=True) * (1.0 / MOBA_BLOCK)
        km = jnp.where((lane == j) | (lane == HEAD_DIM + j), colj, km)
    qrow = lax.broadcasted_iota(jnp.int32, (MOBA_BLOCK, 1), 0)
    for hh in range(2):
        in_l = (lane >= HEAD_DIM * hh) & (lane < HEAD_DIM * (hh + 1))
        in_r = (rowd >= HEAD_DIM * hh) & (rowd < HEAD_DIM * (hh + 1))
        base = HEAD_DIM * (1 - hh)
        ind = jnp.where(rowd - base == keyblk, 1.0, 0.0)
        kp = jnp.where(in_r, kt, ind).astype(BF16)
        jl = lane - base
        is_bias_lane = (jl >= 0) & (jl < nblk)
        for i in range(nblk):
            rows = slice(i * MOBA_BLOCK, (i + 1) * MOBA_BLOCK)
            qi = q_ref[rows, :]
            if i > 0:
                gate = jnp.dot(jnp.where(in_l, qi, 0.0), km, precision=HIGHEST, preferred_element_type=F32)
                rank = jnp.zeros((MOBA_BLOCK, LANES), F32)
                for jp in range(i):
                    cj = gate[:, base + jp:base + jp + 1]
                    beats = (cj > gate) | ((cj == gate) & (jp < jl))
                    rank = rank + jnp.where(beats, 1.0, 0.0)
                keep = ((jl < i) & (rank < min(MOBA_TOPK, nblk - 1))) | (jl == i)
            else:
                keep = jl == 0
            bias = jnp.where(is_bias_lane & jnp.logical_not(keep), NEG, 0.0)
            qa = jnp.where(in_l, qi * ATTN_SCALE, bias).astype(BF16)
            nk = (i + 1) * MOBA_BLOCK
            s = jnp.dot(qa, kp[:, :nk], preferred_element_type=F32)
            kcol = lax.broadcasted_iota(jnp.int32, (1, nk), 1)
            s = jnp.where(kcol - i * MOBA_BLOCK <= qrow, s, NEG)
            m = jnp.max(s, axis=-1, keepdims=True)
            p = jnp.exp(s - m)
            l = jnp.sum(p, axis=-1, keepdims=True)
            o = lax.dot_general(p.astype(BF16), vtb[:, :nk], NT_DIMS, preferred_element_type=F32) / l
            if hh == 0:
                o_ref[rows, :] = o
            else:
                o_ref[rows, :] = jnp.where(in_l, o, o_ref[rows, :])


def _moba_prompt(q, kt, vt):
    B, S, _ = q.shape
    nblk = S // MOBA_BLOCK
    qspec = pl.BlockSpec((None, S, LANES), lambda b, h: (b, 0, h))
    tspec = pl.BlockSpec((None, LANES, S), lambda b, h: (b, h, 0))
    return pl.pallas_call(
        functools.partial(_moba_prompt_kernel, nblk=nblk),
        grid=(B, WIDTH // LANES),
        in_specs=[qspec, tspec, tspec],
        out_specs=qspec,
        out_shape=jax.ShapeDtypeStruct((B, S, WIDTH), F32),
        compiler_params=_cparams(("parallel", "parallel")),
        name="moba_prompt",
    )(q, kt, vt)


PAGES_PER_STEP = 16


def _kmean_kernel(pt_ref, *refs):
    del pt_ref
    pages, km_ref = refs[:PAGES_PER_STEP], refs[PAGES_PER_STEP]
    c = pl.program_id(1)

    @pl.when(c == 0)
    def _():
        km_ref[...] = jnp.zeros(km_ref.shape, F32)

    lane = lax.broadcasted_iota(jnp.int32, (1, 1, LANES), 2)
    blocks_per_step = PAGES_PER_STEP // PAGES_PER_BLOCK
    for jj in range(blocks_per_step):
        tot = pages[PAGES_PER_BLOCK * jj][...]
        for r in range(1, PAGES_PER_BLOCK):
            tot = tot + pages[PAGES_PER_BLOCK * jj + r][...]
        colj = jnp.sum(tot, axis=-1, keepdims=True) * (1.0 / MOBA_BLOCK)
        km_ref[...] = jnp.where(lane == c * blocks_per_step + jj, colj, km_ref[...])


def _kmean(ck, page_table, nblk):
    DB = page_table.shape[0]
    assert nblk <= LANES and (nblk * PAGES_PER_BLOCK) % PAGES_PER_STEP == 0
    page_spec = lambda r: pl.BlockSpec(
        (None, N_HEADS, HEAD_DIM, PAGE_SIZE), lambda b, c, pt: (pt[b, c * PAGES_PER_STEP + r], 0, 0, 0))
    return pl.pallas_call(
        _kmean_kernel,
        grid_spec=pltpu.PrefetchScalarGridSpec(
            num_scalar_prefetch=1,
            grid=(DB, nblk * PAGES_PER_BLOCK // PAGES_PER_STEP),
            in_specs=[page_spec(r) for r in range(PAGES_PER_STEP)],
            out_specs=pl.BlockSpec((None, N_HEADS, HEAD_DIM, LANES), lambda b, c, pt: (b, 0, 0, 0)),
        ),
        out_shape=jax.ShapeDtypeStruct((DB, N_HEADS, HEAD_DIM, LANES), F32),
        compiler_params=_cparams(("parallel", "arbitrary")),
        name="moba_kmean",
    )(page_table, *([ck] * PAGES_PER_STEP))


def _select_kernel(q_ref, km_ref, sel_ref, *, nblk):
    lane = lax.broadcasted_iota(jnp.int32, (1, LANES), 1)
    lanef = lane.astype(F32)
    for h in range(N_HEADS):
        gate = jnp.dot(q_ref[h], km_ref[h], precision=HIGHEST, preferred_element_type=F32)
        gate = jnp.where(lane < nblk, gate, -jnp.inf)
        out = jnp.zeros(gate.shape, F32)
        for r in range(MOBA_TOPK):
            m = jnp.max(gate, axis=-1, keepdims=True)
            idx = jnp.min(jnp.where(gate == m, lanef, float(LANES)), axis=-1, keepdims=True)
            out = jnp.where(lane == r, idx, out)
            gate = jnp.where(lanef == idx, -jnp.inf, gate)
        sel_ref[h] = out.astype(jnp.int32)


def _select(qh, km, nblk):
    DB, H, T, Dh = qh.shape
    return pl.pallas_call(
        functools.partial(_select_kernel, nblk=nblk),
        grid=(DB,),
        in_specs=[pl.BlockSpec((None, H, T, Dh), lambda b: (b, 0, 0, 0)),
                  pl.BlockSpec((None, H, Dh, LANES), lambda b: (b, 0, 0, 0))],
        out_specs=pl.BlockSpec((None, H, T, LANES), lambda b: (b, 0, 0, 0)),
        out_shape=jax.ShapeDtypeStruct((DB, H, T, LANES), jnp.int32),
        compiler_params=_cparams(("parallel",)),
        name="moba_select",
    )(qh, km)


def _sample_attn_kernel(sel_ref, pt_ref, qt_ref, knt_ref, vnt_ref, ck_hbm, cv_hbm, o_ref, kbuf, vbuf, sems,
                        *, T, n_sel):
    b, h = pl.program_id(0), pl.program_id(1)
    nh = pl.num_programs(1)
    step = b * nh + h
    nstep = pl.num_programs(0) * nh
    nkeys = n_sel * MOBA_BLOCK

    def copies(bb, hh, slot, lookup):
        out = []
        for t in range(T):
            for s in range(n_sel):
                blk = sel_ref[((bb * nh + hh) * T + t) * n_sel + s] if lookup else 0
                for r in range(PAGES_PER_BLOCK):
                    page = pt_ref[bb, blk * PAGES_PER_BLOCK + r] if lookup else 0
                    dst = pl.ds((s * PAGES_PER_BLOCK + r) * PAGE_SIZE, PAGE_SIZE)
                    out.append(pltpu.make_async_copy(ck_hbm.at[page, hh], kbuf.at[slot, t, :, dst], sems.at[slot]))
                    out.append(pltpu.make_async_copy(cv_hbm.at[page, hh], vbuf.at[slot, t, :, dst], sems.at[slot]))
        return out

    @pl.when(step == 0)
    def _():
        for cp in copies(b, h, 0, True):
            cp.start()

    @pl.when(step + 1 < nstep)
    def _():
        nxt = step + 1
        for cp in copies(nxt // nh, nxt % nh, nxt % 2, True):
            cp.start()

    slot = step % 2
    for cp in copies(0, 0, slot, False):
        cp.wait()

    qt = qt_ref[...]
    knt = knt_ref[...]
    vnt = vnt_ref[...]
    lane_t = lax.broadcasted_iota(jnp.int32, (1, T), 1)
    out = jnp.zeros((HEAD_DIM, LANES), F32)
    lane = lax.broadcasted_iota(jnp.int32, (1, LANES), 1)
    for t in range(T):
        qc = qt[:, t:t + 1]
        s_past = jnp.sum(kbuf[slot, t] * qc, axis=0, keepdims=True) * ATTN_SCALE
        s_new = jnp.sum(knt * qc, axis=0, keepdims=True) * ATTN_SCALE
        s_new = jnp.where(lane_t <= t, s_new, NEG)
        m = jnp.maximum(jnp.max(s_past, axis=-1, keepdims=True), jnp.max(s_new, axis=-1, keepdims=True))
        p_past = jnp.exp(s_past - m)
        p_new = jnp.exp(s_new - m)
        l = jnp.sum(p_past, axis=-1, keepdims=True) + jnp.sum(p_new, axis=-1, keepdims=True)
        o = (jnp.sum(vbuf[slot, t] * p_past, axis=-1, keepdims=True)
             + jnp.sum(vnt * p_new, axis=-1, keepdims=True)) / l
        out = jnp.where(lane == t, o, out)
    o_ref[...] = out


def _sample_attn(sel_flat, page_table, qt, knt, vnt, ck, cv, n_sel):
    DB, H, Dh, T = qt.shape
    small = pl.BlockSpec((None, None, Dh, T), lambda b, h, *_: (b, h, 0, 0))
    nkeys = n_sel * MOBA_BLOCK
    return pl.pallas_call(
        functools.partial(_sample_attn_kernel, T=T, n_sel=n_sel),
        grid_spec=pltpu.PrefetchScalarGridSpec(
            num_scalar_prefetch=2,
            grid=(DB, H),
            in_specs=[small, small, small, pl.BlockSpec(memory_space=pl.ANY), pl.BlockSpec(memory_space=pl.ANY)],
            out_specs=pl.BlockSpec((None, None, Dh, LANES), lambda b, h, *_: (b, h, 0, 0)),
            scratch_shapes=[pltpu.VMEM((2, T, Dh, nkeys), F32), pltpu.VMEM((2, T, Dh, nkeys), F32),
                            pltpu.SemaphoreType.DMA((2,))],
        ),
        out_shape=jax.ShapeDtypeStruct((DB, H, Dh, LANES), F32),
        compiler_params=_cparams(("arbitrary", "arbitrary")),
        name="moba_sample",
    )(sel_flat, page_table, qt, knt, vnt, ck, cv)


def _moba_sample(q, kt, vt, cache_k, cache_v, page_table):
    DB, n_pages = page_table.shape
    T = q.shape[1] // DB
    nblk = n_pages // PAGES_PER_BLOCK
    n_sel = min(MOBA_TOPK, nblk)
    assert n_pages % PAGES_PER_BLOCK == 0 and n_sel == MOBA_TOPK
    ck = jnp.transpose(cache_k, (0, 2, 3, 1))
    cv = jnp.transpose(cache_v, (0, 2, 3, 1))
    qh = jnp.transpose(q.reshape(DB, T, N_HEADS, HEAD_DIM), (0, 2, 1, 3))
    qt = jnp.transpose(qh, (0, 1, 3, 2))
    knt = jnp.transpose(kt.reshape(N_HEADS, HEAD_DIM, DB, T), (2, 0, 1, 3))
    vnt = jnp.transpose(vt.reshape(N_HEADS, HEAD_DIM, DB, T), (2, 0, 1, 3))
    km = _kmean(ck, page_table, nblk)
    sel = _select(qh, km, nblk)[..., :n_sel].reshape(-1)
    o = _sample_attn(sel, page_table, qt, knt, vnt, ck, cv, n_sel)
    return jnp.transpose(o[..., :T], (0, 3, 1, 2)).reshape(DB * T, WIDTH)


def _rwkv_prep_kernel(pb_ref, prev_ref, mu_ref, w0_ref, w2p_ref, a0_ref, a2p_ref, g2_ref, kk_ref, ka_ref,
                      rk_ref, bo_ref, r_ref, w_ref, k_ref, v_ref, a_ref, b_ref, g_ref, bonus_ref, carry):
    c = pl.program_id(1)

    @pl.when(c == 0)
    def _():
        carry[...] = prev_ref[...]

    pb3 = pb_ref[...]
    G, Tc, P = pb3.shape
    tpos = lax.broadcasted_iota(jnp.int32, (1, Tc, 1), 1)
    prev3 = jnp.where(tpos == 0, carry[...], pltpu.roll(pb3, 1, axis=1))
    carry[...] = pb3[:, Tc - 1:Tc, :]
    pb = pb3.reshape(G * Tc, P)
    xm = pb + (prev3.reshape(G * Tc, P) - pb) * mu_ref[...]
    r = xm[:, 0:WIDTH]
    k = xm[:, WIDTH:2 * WIDTH]
    v = xm[:, 2 * WIDTH:3 * WIDTH]
    wa = xm[:, 3 * WIDTH:3 * WIDTH + D_DECAY_LORA + D_AAA_LORA]
    gd = xm[:, 3 * WIDTH + D_DECAY_LORA + D_AAA_LORA:]
    bo = bo_ref[...]
    z = w0_ref[...] + jnp.dot(jnp.tanh(wa), w2p_ref[...], precision=HIGHEST, preferred_element_type=F32)
    softplus_negz = jnp.maximum(-z, 0.0) + jnp.log(1.0 + jnp.exp(-jnp.abs(z)))
    w_ref[...] = jnp.exp(-jnp.exp(-softplus_negz - 0.5))
    lr = _sigmoid(a0_ref[...] + jnp.dot(wa, a2p_ref[...], precision=HIGHEST, preferred_element_type=F32))
    g_ref[...] = jnp.dot(_sigmoid(gd), g2_ref[...], precision=HIGHEST, preferred_element_type=F32)
    kk = k * kk_ref[...]
    kk = kk / jnp.maximum(jnp.sqrt(_head_sum(kk * kk, bo)), 1e-12)
    k2 = k * (1.0 + (lr - 1.0) * ka_ref[...])
    r_ref[...] = r
    k_ref[...] = k2
    v_ref[...] = v
    a_ref[...] = -kk
    b_ref[...] = kk * lr
    bonus_ref[...] = _head_sum(r * k2 * rk_ref[...], bo) * v


def _rwkv_prep(pb, prev_row, consts, G, Tc):
    B, T, P = pb.shape
    out = jax.ShapeDtypeStruct((B * T, WIDTH), F32)
    nt = T // Tc
    ospec = pl.BlockSpec((G * Tc, WIDTH), lambda b, c: (b * nt + c, 0))
    return pl.pallas_call(
        _rwkv_prep_kernel,
        grid=(B // G, nt),
        in_specs=[pl.BlockSpec((G, Tc, P), lambda b, c: (b, c, 0)),
                  pl.BlockSpec((G, 1, P), lambda b, c: (b, 0, 0))] + [_const_spec(a.shape) for a in consts],
        out_specs=[ospec] * 8,
        out_shape=[out] * 8,
        scratch_shapes=[pltpu.VMEM((G, 1, P), F32)],
        compiler_params=_cparams(("parallel", "arbitrary")),
        name="rwkv_prep",
    )(pb, prev_row, *consts)


def _wkv_kernel(r_ref, w_ref, k_ref, v_ref, a_ref, b_ref, s0_ref, y_ref, sfin_ref, S):
    c = pl.program_id(1)
    Tc, N = r_ref.shape[0], r_ref.shape[1]

    @pl.when(c == 0)
    def _():
        S[...] = s0_ref[...]

    def token(t, carry):
        def rows(ref, jb):
            return ref[t, pl.ds(pl.multiple_of(jb * SUBLANES, SUBLANES), SUBLANES), :]

        def sa_body(jb, sa):
            a8 = rows(a_ref, jb)
            for jj in range(SUBLANES):
                sa = sa + S[jb * SUBLANES + jj] * a8[jj:jj + 1, :]
            return sa

        sa = lax.fori_loop(0, N // SUBLANES, sa_body, jnp.zeros(S.shape[1:], F32))
        vt = v_ref[t]

        def up_body(jb, y):
            w8, b8, k8, r8 = rows(w_ref, jb), rows(b_ref, jb), rows(k_ref, jb), rows(r_ref, jb)
            for jj in range(SUBLANES):
                j = jb * SUBLANES + jj
                s = S[j] * w8[jj:jj + 1, :] + sa * b8[jj:jj + 1, :] + vt * k8[jj:jj + 1, :]
                S[j] = s
                y = y + s * r8[jj:jj + 1, :]
            return y

        y_ref[t] = lax.fori_loop(0, N // SUBLANES, up_body, jnp.zeros(S.shape[1:], F32))
        return carry

    lax.fori_loop(0, Tc, token, 0)

    @pl.when(c == pl.num_programs(1) - 1)
    def _():
        sfin_ref[...] = S[...]


def _wkv(r, w, k, v, a, b, s0, Tc):
    T, N, L = r.shape
    lb = min(LANES, L)
    vec = pl.BlockSpec((Tc, N, lb), lambda g, c: (c, 0, g))
    st = pl.BlockSpec((N, N, lb), lambda g, c: (0, 0, g))
    return pl.pallas_call(
        _wkv_kernel,
        grid=(L // lb, T // Tc),
        in_specs=[vec] * 6 + [st],
        out_specs=[vec, st],
        out_shape=[jax.ShapeDtypeStruct((T, N, L), F32), jax.ShapeDtypeStruct((N, N, L), F32)],
        scratch_shapes=[pltpu.VMEM((N, N, lb), F32)],
        compiler_params=_cparams(("parallel", "arbitrary")),
        name="rwkv_wkv",
    )(r, w, k, v, a, b, s0)


def _rwkv(pb, prev_row, wkv0, consts, G, Tc, Tw):
    B, T, _ = pb.shape
    r, w, k, v, a, b, g, bonus = _rwkv_prep(pb, prev_row[:, None, :], consts, G, Tc)
    to_lanes = lambda x: jnp.transpose(x.reshape(B, T, N_HEADS, HEAD_DIM), (1, 3, 0, 2)).reshape(T, HEAD_DIM, B * N_HEADS)
    s0 = jnp.transpose(wkv0, (3, 2, 0, 1)).reshape(HEAD_DIM, HEAD_DIM, B * N_HEADS)
    y, sfin = _wkv(*(to_lanes(x) for x in (r, w, k, v, a, b)), s0, Tw)
    y = jnp.transpose(y.reshape(T, HEAD_DIM, B, N_HEADS), (2, 0, 3, 1)).reshape(B * T, WIDTH)
    sfin = jnp.transpose(sfin.reshape(HEAD_DIM, HEAD_DIM, B, N_HEADS), (2, 3, 1, 0))
    return y, g, bonus, sfin


def _merge_kernel(x_ref, at_ref, y_ref, bonus_ref, g_ref, sg_ref, lng_ref, lnb_ref, bo_ref,
                  wa_ref, wb_ref, wo_ref, h_ref):
    D = x_ref.shape[1]
    bo = bo_ref[...]
    y = y_ref[...]
    d = y - _head_sum(y, bo) * (1.0 / HEAD_DIM)
    var = _head_sum(d * d, bo) * (1.0 / HEAD_DIM)
    yn = d * lax.rsqrt(var + GN_EPS) * lng_ref[...] + lnb_ref[...]
    ob = (yn + bonus_ref[...]) * g_ref[...]
    y_a = jnp.dot(at_ref[...].astype(BF16), wa_ref[...], preferred_element_type=F32)
    y_b = jnp.dot(ob.astype(BF16), wb_ref[...], preferred_element_type=F32)
    mix = sg_ref[:, 0:D] * y_a + sg_ref[:, D:2 * D] * y_b
    h_ref[...] = x_ref[...] + jnp.dot(mix.astype(BF16), wo_ref[...], preferred_element_type=F32)


def _merge(x, attn, y, bonus, g, sg, lng, lnb, bo, wa, wb, wo, tm):
    N, D = x.shape
    row = lambda w: pl.BlockSpec((tm, w), lambda i: (i, 0))
    consts = (lng, lnb, bo, wa, wb, wo)
    return pl.pallas_call(
        _merge_kernel,
        grid=(N // tm,),
        in_specs=[row(D), row(WIDTH), row(WIDTH), row(WIDTH), row(WIDTH), row(2 * D)]
                 + [_const_spec(a.shape) for a in consts],
        out_specs=row(D),
        out_shape=jax.ShapeDtypeStruct((N, D), F32),
        compiler_params=_cparams(("parallel",)),
        name="merge",
    )(x, attn, y, bonus, g, sg, *consts)


FFN_CHUNK = 256


def _ffn_kernel(h_ref, cp_ref, ln2_ref, wup_ref, cw_ref, cb_ref, wdn_ref, lnf_ref, y_ref, cs_ref, carry, *, dff):
    c = pl.program_id(1)

    @pl.when(c == 0)
    def _():
        carry[...] = cp_ref[...]

    h3 = h_ref[...]
    G, Tc, D = h3.shape
    h = h3.reshape(G * Tc, D)
    xb = _rms(h, ln2_ref[...]).astype(BF16)
    tpos = lax.broadcasted_iota(jnp.int32, (1, Tc, 1), 1)

    def conv(cols):
        u = jnp.dot(xb, wup_ref[:, cols], preferred_element_type=F32).reshape(G, Tc, FFN_CHUNK)
        p0 = carry[:, 0:1, cols]
        p1 = carry[:, 1:2, cols]
        u1 = jnp.where(tpos == 0, p1, pltpu.roll(u, 1, axis=1))
        u2 = jnp.where(tpos == 0, p0, jnp.where(tpos == 1, p1, pltpu.roll(u, 2, axis=1)))
        carry[:, :, cols] = u[:, Tc - (CONV_W - 1):Tc, :]
        cv = cb_ref[:, cols] + u2 * cw_ref[0:1, cols] + u1 * cw_ref[1:2, cols] + u * cw_ref[2:3, cols]
        return cv.reshape(G * Tc, FFN_CHUNK)

    acc = jnp.zeros((G * Tc, D), F32)
    for n in range(dff // FFN_CHUNK):
        val = conv(slice(n * FFN_CHUNK, (n + 1) * FFN_CHUNK))
        gt = conv(slice(dff + n * FFN_CHUNK, dff + (n + 1) * FFN_CHUNK))
        act = (gt * _sigmoid(gt) * val).astype(BF16)
        acc = acc + jnp.dot(act, wdn_ref[n * FFN_CHUNK:(n + 1) * FFN_CHUNK, :], preferred_element_type=F32)
    cs_ref[...] = carry[...]
    y_ref[...] = _rms(h + acc, lnf_ref[...]).reshape(G, Tc, D)


def _ffn(h, conv_prev, ln2, wup, cw, cb, wdn, lnf, G, Tc):
    B, T, D = h.shape
    dff = wdn.shape[0]
    assert dff % FFN_CHUNK == 0 and Tc >= CONV_W - 1 and (G == 1 or Tc == T)
    consts_a = (ln2, wup, cw, cb, wdn, lnf)
    tile = pl.BlockSpec((G, Tc, D), lambda b, c: (b, c, 0))
    cst = pl.BlockSpec((G, CONV_W - 1, 2 * dff), lambda b, c: (b, 0, 0))
    return pl.pallas_call(
        functools.partial(_ffn_kernel, dff=dff),
        grid=(B // G, T // Tc),
        in_specs=[tile, cst] + [_const_spec(a.shape) for a in consts_a],
        out_specs=[tile, cst],
        out_shape=[jax.ShapeDtypeStruct((B, T, D), F32), jax.ShapeDtypeStruct((B, CONV_W - 1, 2 * dff), F32)],
        scratch_shapes=[pltpu.VMEM((G, CONV_W - 1, 2 * dff), F32)],
        compiler_params=_cparams(("parallel", "arbitrary")),
        name="convffn",
    )(h, conv_prev, *consts_a)


def _trunk(x, attend, shift_prev, wkv0, conv_prev, p, tiles):
    B, T, D = x.shape
    q, kt, vt, pb, sg = _inproj(*tiles["inproj_view"](x), p["ln1_g"], p["wq"], p["wkvt"], p["wpb"], p["wg"],
                                tiles["inproj_tm"])
    attn = attend(q, kt, vt)
    pb = pb.reshape(B, T, RWKV_PROJ)
    y, g, bonus, sfin = _rwkv(pb, shift_prev, wkv0, p["rwkv_consts"], tiles["G"], tiles["Tc"], tiles["Tw"])
    h = _merge(x.reshape(B * T, D), attn.reshape(B * T, WIDTH), y, bonus, g, sg.reshape(B * T, 2 * D),
               p["lnx_g"], p["lnx_b"], p["bo"], p["w_br_a"], p["w_br_b"], p["w_o"], tiles["merge_tm"])
    yout, conv_new = _ffn(h.reshape(B, T, D), conv_prev, p["ln2_g"], p["w_up"], p["conv_w"], p["conv_b"],
                          p["w_down"], p["lnf_g"], tiles["G"], tiles["Tc"])
    return yout, kt, vt, sfin, pb[:, -1, :], conv_new


def kernel(x_prompt, x_sample, cache_k, cache_v, page_table, state_wkv, state_tshift, state_conv, ln1_g, w_in, w_br_a, mu, w0, w2, a0, a2, g2, k_k, k_a, r_k, lnx_g, lnx_b, w_br_b, w_o, ln2_g, w_up, conv_w, conv_b, w_down, lnf_g):
    assert w_in.shape[0] == 1, "single-layer trunk"
    BP, S, D = x_prompt.shape
    DB, TS, _ = x_sample.shape
    zrows = jnp.zeros((D_AAA_LORA, WIDTH), F32)
    blk = jnp.arange(LANES) // HEAD_DIM
    p = dict(
        ln1_g=ln1_g, ln2_g=ln2_g, lnf_g=lnf_g[None, :], lnx_g=lnx_g, lnx_b=lnx_b,
        wq=w_in[0, :, 0:WIDTH].astype(BF16),
        wkvt=jnp.transpose(w_in[0, :, WIDTH:3 * WIDTH]).astype(BF16),
        wpb=w_in[0, :, 3 * WIDTH:3 * WIDTH + RWKV_PROJ].astype(BF16),
        wg=w_in[0, :, 3 * WIDTH + RWKV_PROJ:].astype(BF16),
        w_br_a=w_br_a[0].astype(BF16), w_br_b=w_br_b[0].astype(BF16), w_o=w_o[0].astype(BF16),
        w_up=w_up[0].astype(BF16), w_down=w_down[0].astype(BF16), conv_w=conv_w[0], conv_b=conv_b,
        bo=(blk[:, None] == blk[None, :]).astype(BF16),
    )
    p["rwkv_consts"] = (mu, w0, jnp.concatenate([w2[0], zrows], axis=0), a0,
                        jnp.concatenate([zrows, a2[0]], axis=0), g2[0], k_k, k_a,
                        r_k.reshape(1, WIDTH), p["bo"])

    tiles_p = dict(inproj_view=lambda x: (x,), inproj_tm=min(256, S), G=1, Tc=min(256, S), Tw=min(32, S),
                   merge_tm=min(256, BP * S))
    yp, ktp, vtp, wkv_p, tshift_p, conv_p = _trunk(
        x_prompt, lambda q, kt, vt: _moba_prompt(q, kt, vt),
        jnp.zeros((BP, RWKV_PROJ), F32), jnp.zeros((BP, N_HEADS, HEAD_DIM, HEAD_DIM), F32),
        jnp.zeros((BP, CONV_W - 1, w_up.shape[2]), F32), p, tiles_p)

    tiles_s = dict(inproj_view=lambda x: (x.reshape(1, DB * TS, D),), inproj_tm=DB * TS, G=DB, Tc=TS, Tw=TS,
                   merge_tm=DB * TS)
    attend_s = lambda q, kt, vt: _moba_sample(q, kt, vt, cache_k[0], cache_v[0], page_table)
    ys, kts, vts, wkv_s, tshift_s, conv_s = _trunk(
        x_sample, attend_s, state_tshift[0], state_wkv[0], state_conv[0], p, tiles_s)

    heads_p = lambda t: jnp.transpose(t.reshape(BP, N_HEADS, HEAD_DIM, S), (0, 3, 1, 2))[None]
    heads_s = lambda t: jnp.transpose(t.reshape(N_HEADS, HEAD_DIM, DB, TS), (2, 3, 0, 1))[None]
    return (yp, ys, heads_p(ktp), heads_p(vtp), wkv_p[None], tshift_p[None], conv_p[None],
            heads_s(kts), heads_s(vts), wkv_s[None], tshift_s[None], conv_s[None])
```

```python
import functools

import jax
import jax.numpy as jnp
from jax import lax
from jax.experimental import pallas as pl
from jax.experimental.pallas import tpu as pltpu

F32 = jnp.float32
BF16 = jnp.bfloat16
HIGHEST = lax.Precision.HIGHEST
NT_DIMS = (((1,), (1,)), ((), ()))

HEAD_DIM = 64
N_HEADS = 8
WIDTH = N_HEADS * HEAD_DIM
MOBA_BLOCK = 256
MOBA_TOPK = 3
PAGE_SIZE = 128
PAGES_PER_BLOCK = MOBA_BLOCK // PAGE_SIZE
D_DECAY_LORA = 64
D_AAA_LORA = 64
D_GATE_LORA = 128
RWKV_PROJ = 3 * WIDTH + D_DECAY_LORA + D_AAA_LORA + D_GATE_LORA
CONV_W = 3
RMS_EPS = 1e-6
GN_EPS = 64e-5
NEG = -1e30
ATTN_SCALE = HEAD_DIM ** -0.5

LANES = 128
SUBLANES = 8
VMEM_LIMIT = 56 * 1024 * 1024


def _cparams(sem):
    return pltpu.CompilerParams(dimension_semantics=sem, vmem_limit_bytes=VMEM_LIMIT)


def _const_spec(shape):
    nd = len(shape)
    return pl.BlockSpec(shape, lambda *_: (0,) * nd, pipeline_mode=pl.Buffered(1))


def _sigmoid(x):
    return 1.0 / (1.0 + jnp.exp(-x))


def _rms(x, g):
    return x * lax.rsqrt(jnp.mean(x * x, axis=-1, keepdims=True) + RMS_EPS) * g


def _inproj_kernel(x_ref, g_ref, wq_ref, wkvt_ref, wpb_ref, wg_ref,
                   q_ref, kt_ref, vt_ref, pb_ref, sg_ref):
    xb = _rms(x_ref[...], g_ref[...]).astype(BF16)
    q_ref[...] = jnp.dot(xb, wq_ref[...], preferred_element_type=F32)
    kt_ref[...] = lax.dot_general(wkvt_ref[0:WIDTH, :], xb, NT_DIMS, preferred_element_type=F32)
    vt_ref[...] = lax.dot_general(wkvt_ref[WIDTH:2 * WIDTH, :], xb, NT_DIMS, preferred_element_type=F32)
    pb_ref[...] = jnp.dot(xb, wpb_ref[...], preferred_element_type=F32)
    sg_ref[...] = _sigmoid(jnp.dot(xb, wg_ref[...], preferred_element_type=F32))


def _inproj(x, ln_g, wq, wkvt, wpb, wg, tm):
    B, T, D = x.shape
    row = lambda w: pl.BlockSpec((None, tm, w), lambda b, t: (b, t, 0))
    col = pl.BlockSpec((None, WIDTH, tm), lambda b, t: (b, 0, t))
    return pl.pallas_call(
        _inproj_kernel,
        grid=(B, T // tm),
        in_specs=[row(D), _const_spec((1, D)), _const_spec(wq.shape), _const_spec(wkvt.shape),
                  _const_spec(wpb.shape), _const_spec(wg.shape)],
        out_specs=[row(WIDTH), col, col, row(RWKV_PROJ), row(2 * D)],
        out_shape=[jax.ShapeDtypeStruct((B, T, WIDTH), F32),
                   jax.ShapeDtypeStruct((B, WIDTH, T), F32),
                   jax.ShapeDtypeStruct((B, WIDTH, T), F32),
                   jax.ShapeDtypeStruct((B, T, RWKV_PROJ), F32),
                   jax.ShapeDtypeStruct((B, T, 2 * D), F32)],
        compiler_params=_cparams(("parallel", "parallel")),
        name="inproj",
    )(x, ln_g, wq, wkvt, wpb, wg)


def _moba_prompt_kernel(q_ref, kt_ref, vt_ref, o_ref, *, nblk):
    S = q_ref.shape[0]
    kt = kt_ref[...]
    vtb = vt_ref[...].astype(BF16)
    lane = lax.broadcasted_iota(jnp.int32, (1, LANES), 1)
    rowd = lax.broadcasted_iota(jnp.int32, (LANES, 1), 0)
    keyblk = lax.broadcasted_iota(jnp.int32, (1, S), 1) // MOBA_BLOCK
    km = jnp.zeros((LANES, LANES), F32)
    for j in range(nblk):
        colj = jnp.sum(kt[:, j * MOBA_BLOCK:(j + 1) * MOBA_BLOCK], axis=1, keepdims=True) * (1.0 / MOBA_BLOCK)
        km = jnp.where((lane == j) | (lane == HEAD_DIM + j), colj, km)
    qrow = lax.broadcasted_iota(jnp.int32, (MOBA_BLOCK, 1), 0)
    kcol = lax.broadcasted_iota(jnp.int32, (1, MOBA_BLOCK), 1)
    for hh in range(2):
        in_l = (lane >= HEAD_DIM * hh) & (lane < HEAD_DIM * (hh + 1))
        in_r = (rowd >= HEAD_DIM * hh) & (rowd < HEAD_DIM * (hh + 1))
        base = HEAD_DIM * (1 - hh)
        ind = jnp.where(rowd - base == keyblk, 1.0, 0.0)
        kp = jnp.where(in_r, kt, ind).astype(BF16)
        jl = lane - base
        is_bias_lane = (jl >= 0) & (jl < nblk)
        for i in range(nblk):
            rows = slice(i * MOBA_BLOCK, (i + 1) * MOBA_BLOCK)
            qi = q_ref[rows, :]
            if i > 0:
                gate = jnp.dot(jnp.where(in_l, qi, 0.0), km, precision=HIGHEST, preferred_element_type=F32)
                rank = jnp.zeros((MOBA_BLOCK, LANES), F32)
                for jp in range(i):
                    cj = gate[:, base + jp:base + jp + 1]
                    beats = (cj > gate) | ((cj == gate) & (jp < jl))
                    rank = rank + jnp.where(beats, 1.0, 0.0)
                keep = ((jl < i) & (rank < min(MOBA_TOPK, nblk - 1))) | (jl == i)
            else:
                keep = jl == 0
            bias = jnp.where(is_bias_lane & jnp.logical_not(keep), NEG, 0.0)
            qa = jnp.where(in_l, qi * ATTN_SCALE, bias).astype(BF16)
            own = slice(i * MOBA_BLOCK, (i + 1) * MOBA_BLOCK)
            s_own = jnp.dot(qa, kp[:, own], preferred_element_type=F32)
            s_own = jnp.where(kcol <= qrow, s_own, NEG)
            m = jnp.max(s_own, axis=-1, keepdims=True)
            if i > 0:
                past = slice(0, i * MOBA_BLOCK)
                s_past = jnp.dot(qa, kp[:, past], preferred_element_type=F32)
                m = jnp.maximum(m, jnp.max(s_past, axis=-1, keepdims=True))
            p_own = jnp.exp(s_own - m)
            l = jnp.sum(p_own, axis=-1, keepdims=True)
            o = lax.dot_general(p_own.astype(BF16), vtb[:, own], NT_DIMS, preferred_element_type=F32)
            if i > 0:
                p_past = jnp.exp(s_past - m)
                l = l + jnp.sum(p_past, axis=-1, keepdims=True)
                o = o + lax.dot_general(p_past.astype(BF16), vtb[:, past], NT_DIMS, preferred_element_type=F32)
            o = o / l
            if hh == 0:
                o_ref[rows, :] = o
            else:
                o_ref[rows, :] = jnp.where(in_l, o, o_ref[rows, :])


def _moba_prompt(q, kt, vt):
    B, S, _ = q.shape
    nblk = S // MOBA_BLOCK
    qspec = pl.BlockSpec((None, S, LANES), lambda b, h: (b, 0, h))
    tspec = pl.BlockSpec((None, LANES, S), lambda b, h: (b, h, 0))
    return pl.pallas_call(
        functools.partial(_moba_prompt_kernel, nblk=nblk),
        grid=(B, WIDTH // LANES),
        in_specs=[qspec, tspec, tspec],
        out_specs=qspec,
        out_shape=jax.ShapeDtypeStruct((B, S, WIDTH), F32),
        compiler_params=_cparams(("parallel", "parallel")),
        name="moba_prompt",
    )(q, kt, vt)


PAGES_PER_STEP = 16


def _kmean_kernel(pt_ref, *refs):
    del pt_ref
    pages, km_ref = refs[:PAGES_PER_STEP], refs[PAGES_PER_STEP]
    c = pl.program_id(1)

    @pl.when(c == 0)
    def _():
        km_ref[...] = jnp.zeros(km_ref.shape, F32)

    lane = lax.broadcasted_iota(jnp.int32, (1, 1, LANES), 2)
    blocks_per_step = PAGES_PER_STEP // PAGES_PER_BLOCK
    for jj in range(blocks_per_step):
        tot = pages[PAGES_PER_BLOCK * jj][...]
        for r in range(1, PAGES_PER_BLOCK):
            tot = tot + pages[PAGES_PER_BLOCK * jj + r][...]
        colj = jnp.sum(tot, axis=-1, keepdims=True) * (1.0 / MOBA_BLOCK)
        km_ref[...] = jnp.where(lane == c * blocks_per_step + jj, colj, km_ref[...])


def _kmean(ck, page_table, nblk):
    DB = page_table.shape[0]
    assert nblk <= LANES and (nblk * PAGES_PER_BLOCK) % PAGES_PER_STEP == 0
    page_spec = lambda r: pl.BlockSpec(
        (None, N_HEADS, HEAD_DIM, PAGE_SIZE), lambda b, c, pt: (pt[b, c * PAGES_PER_STEP + r], 0, 0, 0))
    return pl.pallas_call(
        _kmean_kernel,
        grid_spec=pltpu.PrefetchScalarGridSpec(
            num_scalar_prefetch=1,
            grid=(DB, nblk * PAGES_PER_BLOCK // PAGES_PER_STEP),
            in_specs=[page_spec(r) for r in range(PAGES_PER_STEP)],
            out_specs=pl.BlockSpec((None, N_HEADS, HEAD_DIM, LANES), lambda b, c, pt: (b, 0, 0, 0)),
        ),
        out_shape=jax.ShapeDtypeStruct((DB, N_HEADS, HEAD_DIM, LANES), F32),
        compiler_params=_cparams(("parallel", "arbitrary")),
        name="moba_kmean",
    )(page_table, *([ck] * PAGES_PER_STEP))


def _select_kernel(q_ref, km_ref, sel_ref, *, nblk):
    lane = lax.broadcasted_iota(jnp.int32, (1, LANES), 1)
    lanef = lane.astype(F32)
    for h in range(N_HEADS):
        gate = jnp.dot(q_ref[h], km_ref[h], precision=HIGHEST, preferred_element_type=F32)
        gate = jnp.where(lane < nblk, gate, -jnp.inf)
        out = jnp.zeros(gate.shape, F32)
        for r in range(MOBA_TOPK):
            m = jnp.max(gate, axis=-1, keepdims=True)
            idx = jnp.min(jnp.where(gate == m, lanef, float(LANES)), axis=-1, keepdims=True)
            out = jnp.where(lane == r, idx, out)
            gate = jnp.where(lanef == idx, -jnp.inf, gate)
        sel_ref[h] = out.astype(jnp.int32)


def _select(qh, km, nblk):
    DB, H, T, Dh = qh.shape
    return pl.pallas_call(
        functools.partial(_select_kernel, nblk=nblk),
        grid=(DB,),
        in_specs=[pl.BlockSpec((None, H, T, Dh), lambda b: (b, 0, 0, 0)),
                  pl.BlockSpec((None, H, Dh, LANES), lambda b: (b, 0, 0, 0))],
        out_specs=pl.BlockSpec((None, H, T, LANES), lambda b: (b, 0, 0, 0)),
        out_shape=jax.ShapeDtypeStruct((DB, H, T, LANES), jnp.int32),
        compiler_params=_cparams(("parallel",)),
        name="moba_select",
    )(qh, km)


def _sample_attn_kernel(sel_ref, pt_ref, qt_ref, knt_ref, vnt_ref, ck_hbm, cv_hbm, o_ref, kbuf, vbuf, sems,
                        *, T, n_sel):
    b, h = pl.program_id(0), pl.program_id(1)
    nh = pl.num_programs(1)
    step = b * nh + h
    nstep = pl.num_programs(0) * nh
    nkeys = n_sel * MOBA_BLOCK

    def copies(bb, hh, slot, lookup):
        out = []
        for t in range(T):
            for s in range(n_sel):
                blk = sel_ref[((bb * nh + hh) * T + t) * n_sel + s] if lookup else 0
                for r in range(PAGES_PER_BLOCK):
                    page = pt_ref[bb, blk * PAGES_PER_BLOCK + r] if lookup else 0
                    dst = pl.ds((s * PAGES_PER_BLOCK + r) * PAGE_SIZE, PAGE_SIZE)
                    out.append(pltpu.make_async_copy(ck_hbm.at[page, hh], kbuf.at[slot, t, :, dst], sems.at[slot]))
                    out.append(pltpu.make_async_copy(cv_hbm.at[page, hh], vbuf.at[slot, t, :, dst], sems.at[slot]))
        return out

    @pl.when(step == 0)
    def _():
        for cp in copies(b, h, 0, True):
            cp.start()

    @pl.when(step + 1 < nstep)
    def _():
        nxt = step + 1
        for cp in copies(nxt // nh, nxt % nh, nxt % 2, True):
            cp.start()

    slot = step % 2
    for cp in copies(0, 0, slot, False):
        cp.wait()

    qt = qt_ref[...]
    knt = knt_ref[...]
    vnt = vnt_ref[...]
    lane_t = lax.broadcasted_iota(jnp.int32, (1, T), 1)
    out = jnp.zeros((HEAD_DIM, LANES), F32)
    lane = lax.broadcasted_iota(jnp.int32, (1, LANES), 1)
    for t in range(T):
        qc = qt[:, t:t + 1]
        s_past = jnp.sum(kbuf[slot, t] * qc, axis=0, keepdims=True) * ATTN_SCALE
        s_new = jnp.sum(knt * qc, axis=0, keepdims=True) * ATTN_SCALE
        s_new = jnp.where(lane_t <= t, s_new, NEG)
        m = jnp.maximum(jnp.max(s_past, axis=-1, keepdims=True), jnp.max(s_new, axis=-1, keepdims=True))
        p_past = jnp.exp(s_past - m)
        p_new = jnp.exp(s_new - m)
        l = jnp.sum(p_past, axis=-1, keepdims=True) + jnp.sum(p_new, axis=-1, keepdims=True)
        o = (jnp.sum(vbuf[slot, t] * p_past, axis=-1, keepdims=True)
             + jnp.sum(vnt * p_new, axis=-1, keepdims=True)) / l
        out = jnp.where(lane == t, o, out)
    o_ref[...] = out


def _sample_attn(sel_flat, page_table, qt, knt, vnt, ck, cv, n_sel):
    DB, H, Dh, T = qt.shape
    small = pl.BlockSpec((None, None, Dh, T), lambda b, h, *_: (b, h, 0, 0))
    nkeys = n_sel * MOBA_BLOCK
    return pl.pallas_call(
        functools.partial(_sample_attn_kernel, T=T, n_sel=n_sel),
        grid_spec=pltpu.PrefetchScalarGridSpec(
            num_scalar_prefetch=2,
            grid=(DB, H),
            in_specs=[small, small, small, pl.BlockSpec(memory_space=pl.ANY), pl.BlockSpec(memory_space=pl.ANY)],
            out_specs=pl.BlockSpec((None, None, Dh, LANES), lambda b, h, *_: (b, h, 0, 0)),
            scratch_shapes=[pltpu.VMEM((2, T, Dh, nkeys), F32), pltpu.VMEM((2, T, Dh, nkeys), F32),
                            pltpu.SemaphoreType.DMA((2,))],
        ),
        out_shape=jax.ShapeDtypeStruct((DB, H, Dh, LANES), F32),
        compiler_params=_cparams(("arbitrary", "arbitrary")),
        name="moba_sample",
    )(sel_flat, page_table, qt, knt, vnt, ck, cv)


def _moba_sample(q, kt, vt, cache_k, cache_v, page_table):
    DB, n_pages = page_table.shape
    T = q.shape[1] // DB
    nblk = n_pages // PAGES_PER_BLOCK
    n_sel = min(MOBA_TOPK, nblk)
    assert n_pages % PAGES_PER_BLOCK == 0 and n_sel == MOBA_TOPK
    ck = jnp.transpose(cache_k, (0, 2, 3, 1))
    cv = jnp.transpose(cache_v, (0, 2, 3, 1))
    qh = jnp.transpose(q.reshape(DB, T, N_HEADS, HEAD_DIM), (0, 2, 1, 3))
    qt = jnp.transpose(qh, (0, 1, 3, 2))
    knt = jnp.transpose(kt.reshape(N_HEADS, HEAD_DIM, DB, T), (2, 0, 1, 3))
    vnt = jnp.transpose(vt.reshape(N_HEADS, HEAD_DIM, DB, T), (2, 0, 1, 3))
    km = _kmean(ck, page_table, nblk)
    sel = _select(qh, km, nblk)[..., :n_sel].reshape(-1)
    o = _sample_attn(sel, page_table, qt, knt, vnt, ck, cv, n_sel)
    return jnp.transpose(o[..., :T], (0, 3, 1, 2)).reshape(DB * T, WIDTH)


def _head_sum(x):
    z = x[:, 0:LANES]
    for p in range(1, x.shape[1] // LANES):
        z = z + x[:, p * LANES:(p + 1) * LANES]
    shift = N_HEADS
    while shift < LANES:
        z = z + pltpu.roll(z, shift, axis=1)
        shift *= 2
    return jnp.concatenate([z] * (x.shape[1] // LANES), axis=1)


def _rwkv_prep_kernel(pb_ref, prev_ref, mu_ref, w0_ref, w2p_ref, a0_ref, a2p_ref, g2_ref, kk_ref, ka_ref,
                      rk_ref, r_ref, w_ref, k_ref, v_ref, a_ref, b_ref, g_ref, bonus_ref, carry, *, transposed):
    c = pl.program_id(1)

    @pl.when(c == 0)
    def _():
        carry[...] = prev_ref[...]

    pb3 = pb_ref[...]
    G, Tc, P = pb3.shape
    tpos = lax.broadcasted_iota(jnp.int32, (1, Tc, 1), 1)
    prev3 = jnp.where(tpos == 0, carry[...], pltpu.roll(pb3, 1, axis=1))
    carry[...] = pb3[:, Tc - 1:Tc, :]
    pb = pb3.reshape(G * Tc, P)
    xm = pb + (prev3.reshape(G * Tc, P) - pb) * mu_ref[...]
    r = xm[:, 0:WIDTH]
    k = xm[:, WIDTH:2 * WIDTH]
    v = xm[:, 2 * WIDTH:3 * WIDTH]
    wa = xm[:, 3 * WIDTH:3 * WIDTH + D_DECAY_LORA + D_AAA_LORA]
    gd = xm[:, 3 * WIDTH + D_DECAY_LORA + D_AAA_LORA:]
    z = w0_ref[...] + jnp.dot(jnp.tanh(wa), w2p_ref[...], precision=HIGHEST, preferred_element_type=F32)
    softplus_negz = jnp.maximum(-z, 0.0) + jnp.log(1.0 + jnp.exp(-jnp.abs(z)))
    w = jnp.exp(-jnp.exp(-softplus_negz - 0.5))
    lr = _sigmoid(a0_ref[...] + jnp.dot(wa, a2p_ref[...], precision=HIGHEST, preferred_element_type=F32))
    g_ref[...] = jnp.dot(_sigmoid(gd), g2_ref[...], precision=HIGHEST, preferred_element_type=F32)
    kk = k * kk_ref[...]
    kk = kk / jnp.maximum(jnp.sqrt(_head_sum(kk * kk)), 1e-12)
    k2 = k * (1.0 + (lr - 1.0) * ka_ref[...])
    bonus_ref[...] = _head_sum(r * k2 * rk_ref[...]) * v
    outs = ((r_ref, r), (w_ref, w), (k_ref, k2), (v_ref, v), (a_ref, -kk), (b_ref, kk * lr))
    for ref, val in outs:
        ref[...] = val.T if transposed else val


def _rwkv_prep(pb, prev_row, consts, G, Tc, transposed):
    B, T, P = pb.shape
    nt = T // Tc
    assert G == 1 or nt == 1
    nat = jax.ShapeDtypeStruct((B * T, WIDTH), F32)
    nat_spec = pl.BlockSpec((G * Tc, WIDTH), lambda b, c: (b * nt + c, 0))
    if transposed:
        assert G == 1
        vec, vec_spec = jax.ShapeDtypeStruct((B, WIDTH, T), F32), pl.BlockSpec((None, WIDTH, Tc), lambda b, c: (b, 0, c))
    else:
        vec, vec_spec = nat, nat_spec
    return pl.pallas_call(
        functools.partial(_rwkv_prep_kernel, transposed=transposed),
        grid=(B // G, nt),
        in_specs=[pl.BlockSpec((G, Tc, P), lambda b, c: (b, c, 0)),
                  pl.BlockSpec((G, 1, P), lambda b, c: (b, 0, 0))] + [_const_spec(a.shape) for a in consts],
        out_specs=[vec_spec] * 6 + [nat_spec] * 2,
        out_shape=[vec] * 6 + [nat] * 2,
        scratch_shapes=[pltpu.VMEM((G, 1, P), F32)],
        compiler_params=_cparams(("parallel", "arbitrary")),
        name="rwkv_prep",
    )(pb, prev_row, *consts)


LANE_CHUNK = 128


def _to_lanes_kernel(x_ref, o_ref, *, tw):
    nb = x_ref.shape[0]
    for j in range(HEAD_DIM):
        m = x_ref[:, j * N_HEADS:(j + 1) * N_HEADS, :].reshape(nb * N_HEADS, LANE_CHUNK)
        mt = m.T
        for q in range(LANE_CHUNK // tw):
            o_ref[q, j] = mt[q * tw:(q + 1) * tw, :]


def _to_lanes(xt, tw):
    B, W, T = xt.shape
    L = B * N_HEADS
    assert L <= LANES and T % LANE_CHUNK == 0 and LANE_CHUNK % tw == 0
    return pl.pallas_call(
        functools.partial(_to_lanes_kernel, tw=tw),
        grid=(T // LANE_CHUNK,),
        in_specs=[pl.BlockSpec((B, W, LANE_CHUNK), lambda c: (0, 0, c))],
        out_specs=pl.BlockSpec((LANE_CHUNK // tw, HEAD_DIM, tw, L), lambda c: (c, 0, 0, 0)),
        out_shape=jax.ShapeDtypeStruct((T // tw, HEAD_DIM, tw, L), F32),
        compiler_params=_cparams(("parallel",)),
        name="to_lanes",
    )(xt)


def _from_lanes_kernel(y_ref, o_ref):
    nb = o_ref.shape[0]
    for i in range(HEAD_DIM):
        m = y_ref[pl.ds(i, LANE_CHUNK, stride=HEAD_DIM), :]
        o_ref[:, i * N_HEADS:(i + 1) * N_HEADS, :] = m.T.reshape(nb, N_HEADS, LANE_CHUNK)


def _from_lanes(y2d, B):
    TN, L = y2d.shape
    T = TN // HEAD_DIM
    assert L == B * N_HEADS <= LANES and T % LANE_CHUNK == 0
    return pl.pallas_call(
        _from_lanes_kernel,
        grid=(T // LANE_CHUNK,),
        in_specs=[pl.BlockSpec((LANE_CHUNK * HEAD_DIM, L), lambda c: (c, 0))],
        out_specs=pl.BlockSpec((B, WIDTH, LANE_CHUNK), lambda c: (0, 0, c)),
        out_shape=jax.ShapeDtypeStruct((B, WIDTH, T), F32),
        compiler_params=_cparams(("parallel",)),
        name="from_lanes",
    )(y2d)


def _wkv_kernel(r_ref, w_ref, k_ref, v_ref, a_ref, b_ref, s0_ref, y_ref, sfin_ref, S):
    c = pl.program_id(1)
    N, Tw = r_ref.shape[0], r_ref.shape[1]

    @pl.when(c == 0)
    def _():
        S[...] = s0_ref[...]

    def token_group(tb, carry):
        base = pl.multiple_of(tb * SUBLANES, SUBLANES)
        for tt in range(SUBLANES):
            row = lambda ref, j: ref[j, pl.ds(base, SUBLANES), :][tt:tt + 1, :]

            def sa_body(jb, sa):
                for jj in range(SUBLANES):
                    j = jb * SUBLANES + jj
                    sa = sa + S[j] * row(a_ref, j)
                return sa

            sa = lax.fori_loop(0, N // SUBLANES, sa_body, jnp.zeros(S.shape[1:], F32))
            vt = v_ref[pl.ds(base + tt, N, stride=Tw), :]

            def up_body(jb, y):
                for jj in range(SUBLANES):
                    j = jb * SUBLANES + jj
                    s = S[j] * row(w_ref, j) + sa * row(b_ref, j) + vt * row(k_ref, j)
                    S[j] = s
                    y = y + s * row(r_ref, j)
                return y

            y_ref[base + tt] = lax.fori_loop(0, N // SUBLANES, up_body, jnp.zeros(S.shape[1:], F32))
        return carry

    lax.fori_loop(0, Tw // SUBLANES, token_group, 0)

    @pl.when(c == pl.num_programs(1) - 1)
    def _():
        sfin_ref[...] = S[...]


def _wkv(r, w, k, v, a, b, s0):
    nc, N, Tw, L = r.shape
    lb = min(LANES, L)
    vec = pl.BlockSpec((None, N, Tw, lb), lambda g, c: (c, 0, 0, g))
    st = pl.BlockSpec((N, N, lb), lambda g, c: (0, 0, g))
    return pl.pallas_call(
        _wkv_kernel,
        grid=(L // lb, nc),
        in_specs=[vec, vec, vec, pl.BlockSpec((N * Tw, lb), lambda g, c: (c, g)), vec, vec, st],
        out_specs=[pl.BlockSpec((Tw, N, lb), lambda g, c: (c, 0, g)), st],
        out_shape=[jax.ShapeDtypeStruct((nc * Tw, N, L), F32), jax.ShapeDtypeStruct((N, N, L), F32)],
        scratch_shapes=[pltpu.VMEM((N, N, lb), F32)],
        compiler_params=_cparams(("parallel", "arbitrary")),
        name="rwkv_wkv",
    )(r, w, k, v.reshape(nc * N * Tw, L), a, b, s0)


def _rwkv(pb, prev_row, wkv0, consts, G, Tc, Tw):
    B, T, _ = pb.shape
    L = B * N_HEADS
    big = T % LANE_CHUNK == 0
    r, w, k, v, a, b, g, bonus = _rwkv_prep(pb, prev_row[:, None, :], consts, G, Tc, big)
    if big:
        coef = [_to_lanes(x, Tw) for x in (r, w, k, v, a, b)]
    else:
        assert Tw == T
        coef = [jnp.transpose(x.reshape(B, T, HEAD_DIM, N_HEADS), (2, 1, 0, 3)).reshape(1, HEAD_DIM, T, L)
                for x in (r, w, k, v, a, b)]
    s0 = jnp.transpose(wkv0, (3, 2, 0, 1)).reshape(HEAD_DIM, HEAD_DIM, L)
    y, sfin = _wkv(*coef, s0)
    if big:
        y = _from_lanes(y.reshape(T * HEAD_DIM, L), B)
    else:
        y = jnp.transpose(y.reshape(T, HEAD_DIM, B, N_HEADS), (2, 0, 1, 3)).reshape(B * T, WIDTH)
    sfin = jnp.transpose(sfin.reshape(HEAD_DIM, HEAD_DIM, B, N_HEADS), (2, 3, 1, 0))
    return y, g, bonus, sfin


def _merge_kernel(x_ref, at_ref, y_ref, bonus_ref, g_ref, sg_ref, lng_ref, lnb_ref,
                  wa_ref, wb_ref, wo_ref, h_ref, *, y_transposed):
    D = x_ref.shape[1]
    y = y_ref[...].T if y_transposed else y_ref[...]
    d = y - _head_sum(y) * (1.0 / HEAD_DIM)
    var = _head_sum(d * d) * (1.0 / HEAD_DIM)
    yn = d * lax.rsqrt(var + GN_EPS) * lng_ref[...] + lnb_ref[...]
    ob = (yn + bonus_ref[...]) * g_ref[...]
    y_a = jnp.dot(at_ref[...].astype(BF16), wa_ref[...], preferred_element_type=F32)
    y_b = jnp.dot(ob.astype(BF16), wb_ref[...], preferred_element_type=F32)
    mix = sg_ref[:, 0:D] * y_a + sg_ref[:, D:2 * D] * y_b
    h_ref[...] = x_ref[...] + jnp.dot(mix.astype(BF16), wo_ref[...], preferred_element_type=F32)


def _merge(x, attn, y, bonus, g, sg, lng, lnb, wa, wb, wo, tm, y_transposed):
    B, T, D = x.shape
    row = lambda w: pl.BlockSpec((None, tm, w), lambda b, t: (b, t, 0))
    yspec = pl.BlockSpec((None, WIDTH, tm), lambda b, t: (b, 0, t)) if y_transposed else row(WIDTH)
    consts = (lng, lnb, wa, wb, wo)
    return pl.pallas_call(
        functools.partial(_merge_kernel, y_transposed=y_transposed),
        grid=(B, T // tm),
        in_specs=[row(D), row(WIDTH), yspec, row(WIDTH), row(WIDTH), row(2 * D)]
                 + [_const_spec(a.shape) for a in consts],
        out_specs=row(D),
        out_shape=jax.ShapeDtypeStruct((B, T, D), F32),
        compiler_params=_cparams(("parallel", "parallel")),
        name="merge",
    )(x, attn, y, bonus, g, sg, *consts)


FFN_CHUNK = 256


def _ffn_kernel(h_ref, cp_ref, ln2_ref, wup_ref, cw_ref, cb_ref, wdn_ref, lnf_ref, y_ref, cs_ref, carry, act_ref,
                *, dff):
    c = pl.program_id(1)

    @pl.when(c == 0)
    def _():
        carry[...] = cp_ref[...]

    h3 = h_ref[...]
    G, Tc, D = h3.shape
    h = h3.reshape(G * Tc, D)
    xb = _rms(h, ln2_ref[...]).astype(BF16)
    thead = lax.broadcasted_iota(jnp.int32, (1, SUBLANES, 1), 1)

    def conv(cols):
        u = jnp.dot(xb, wup_ref[:, cols], preferred_element_type=F32).reshape(G, Tc, FFN_CHUNK)
        w0, w1, w2, cb = cw_ref[0:1, cols], cw_ref[1:2, cols], cw_ref[2:3, cols], cb_ref[:, cols]
        r1 = pltpu.roll(u, 1, axis=1)
        r2 = pltpu.roll(u, 2, axis=1)
        p0 = carry[:, 0:1, cols]
        p1 = carry[:, 1:2, cols]
        h1 = jnp.where(thead == 0, p1, r1[:, 0:SUBLANES, :])
        h2 = jnp.where(thead == 0, p0, jnp.where(thead == 1, p1, r2[:, 0:SUBLANES, :]))
        cv = cb + h2 * w0 + h1 * w1 + u[:, 0:SUBLANES, :] * w2
        if Tc > SUBLANES:
            rest = slice(SUBLANES, Tc)
            cv = jnp.concatenate([cv, cb + r2[:, rest, :] * w0 + r1[:, rest, :] * w1 + u[:, rest, :] * w2], axis=1)
        carry[:, :, cols] = u[:, Tc - (CONV_W - 1):Tc, :]
        return cv.reshape(G * Tc, FFN_CHUNK)

    for n in range(dff // FFN_CHUNK):
        val = conv(slice(n * FFN_CHUNK, (n + 1) * FFN_CHUNK))
        gt = conv(slice(dff + n * FFN_CHUNK, dff + (n + 1) * FFN_CHUNK))
        act_ref[:, n * FFN_CHUNK:(n + 1) * FFN_CHUNK] = (gt * _sigmoid(gt) * val).astype(BF16)
    cs_ref[...] = carry[...]
    out = h + jnp.dot(act_ref[...], wdn_ref[...], preferred_element_type=F32)
    y_ref[...] = _rms(out, lnf_ref[...]).reshape(G, Tc, D)


def _ffn(h, conv_prev, ln2, wup, cw, cb, wdn, lnf, G, Tc):
    B, T, D = h.shape
    dff = wdn.shape[0]
    assert dff % FFN_CHUNK == 0 and Tc >= CONV_W - 1 and (G == 1 or Tc == T)
    consts_a = (ln2, wup, cw, cb, wdn, lnf)
    tile = pl.BlockSpec((G, Tc, D), lambda b, c: (b, c, 0))
    cst = pl.BlockSpec((G, CONV_W - 1, 2 * dff), lambda b, c: (b, 0, 0))
    return pl.pallas_call(
        functools.partial(_ffn_kernel, dff=dff),
        grid=(B // G, T // Tc),
        in_specs=[tile, cst] + [_const_spec(a.shape) for a in consts_a],
        out_specs=[tile, cst],
        out_shape=[jax.ShapeDtypeStruct((B, T, D), F32), jax.ShapeDtypeStruct((B, CONV_W - 1, 2 * dff), F32)],
        scratch_shapes=[pltpu.VMEM((G, CONV_W - 1, 2 * dff), F32), pltpu.VMEM((G * Tc, dff), BF16)],
        compiler_params=_cparams(("parallel", "arbitrary")),
        name="convffn",
    )(h, conv_prev, *consts_a)


def _trunk(x, attend, shift_prev, wkv0, conv_prev, p, tiles):
    B, T, D = x.shape
    Bv, Tv = tiles["rows"]
    q, kt, vt, pb, sg = _inproj(x.reshape(Bv, Tv, D), p["ln1_g"], p["wq"], p["wkvt"], p["wpb"], p["wg"], tiles["tm"])
    attn = attend(q, kt, vt).reshape(Bv, Tv, WIDTH)
    pb = pb.reshape(B, T, RWKV_PROJ)
    y, g, bonus, sfin = _rwkv(pb, shift_prev[:, p["shift_cols"]], wkv0, p["rwkv_consts"],
                              tiles["G"], tiles["Tc"], tiles["Tw"])
    y_transposed = y.ndim == 3
    if not y_transposed:
        y = y.reshape(Bv, Tv, WIDTH)
    h = _merge(x.reshape(Bv, Tv, D), attn, y, bonus.reshape(Bv, Tv, WIDTH), g.reshape(Bv, Tv, WIDTH), sg,
               p["lnx_g"], p["lnx_b"], p["w_br_a"], p["w_br_b"], p["w_o"], tiles["tm"], y_transposed)
    yout, conv_new = _ffn(h.reshape(B, T, D), conv_prev, p["ln2_g"], p["w_up"], p["conv_w"], p["conv_b"],
                          p["w_down"], p["lnf_g"], tiles["G"], tiles["Tc"])
    return yout, kt, vt, sfin, pb[:, -1, :][:, p["unshift_cols"]], conv_new


def kernel(x_prompt, x_sample, cache_k, cache_v, page_table, state_wkv, state_tshift, state_conv, ln1_g, w_in, w_br_a, mu, w0, w2, a0, a2, g2, k_k, k_a, r_k, lnx_g, lnx_b, w_br_b, w_o, ln2_g, w_up, conv_w, conv_b, w_down, lnf_g):
    assert w_in.shape[0] == 1, "single-layer trunk"
    BP, S, D = x_prompt.shape
    DB, TS, _ = x_sample.shape
    zrows = jnp.zeros((D_AAA_LORA, WIDTH), F32)
    lane = jnp.arange(WIDTH)
    perm = (lane % N_HEADS) * HEAD_DIM + lane // N_HEADS
    inv = jnp.argsort(perm)
    tail = jnp.arange(3 * WIDTH, RWKV_PROJ)
    shift_cols = jnp.concatenate([perm, WIDTH + perm, 2 * WIDTH + perm, tail])
    unshift_cols = jnp.concatenate([inv, WIDTH + inv, 2 * WIDTH + inv, tail])
    wpb = w_in[0, :, 3 * WIDTH:3 * WIDTH + RWKV_PROJ][:, shift_cols]
    p = dict(
        ln1_g=ln1_g, ln2_g=ln2_g, lnf_g=lnf_g[None, :], lnx_g=lnx_g[:, perm], lnx_b=lnx_b[:, perm],
        wq=w_in[0, :, 0:WIDTH].astype(BF16),
        wkvt=jnp.transpose(w_in[0, :, WIDTH:3 * WIDTH]).astype(BF16),
        wpb=wpb.astype(BF16),
        wg=w_in[0, :, 3 * WIDTH + RWKV_PROJ:].astype(BF16),
        w_br_a=w_br_a[0].astype(BF16), w_br_b=w_br_b[0][perm, :].astype(BF16), w_o=w_o[0].astype(BF16),
        w_up=w_up[0].astype(BF16), w_down=w_down[0].astype(BF16), conv_w=conv_w[0], conv_b=conv_b,
        shift_cols=shift_cols, unshift_cols=unshift_cols,
    )
    p["rwkv_consts"] = (mu[:, shift_cols], w0[:, perm], jnp.concatenate([w2[0], zrows], axis=0)[:, perm],
                        a0[:, perm], jnp.concatenate([zrows, a2[0]], axis=0)[:, perm], g2[0][:, perm],
                        k_k[:, perm], k_a[:, perm], r_k.reshape(1, WIDTH)[:, perm])

    tiles_p = dict(rows=(BP, S), tm=min(512, S), G=1, Tc=min(512, S), Tw=min(32, S))
    yp, ktp, vtp, wkv_p, tshift_p, conv_p = _trunk(
        x_prompt, lambda q, kt, vt: _moba_prompt(q, kt, vt),
        jnp.zeros((BP, RWKV_PROJ), F32), jnp.zeros((BP, N_HEADS, HEAD_DIM, HEAD_DIM), F32),
        jnp.zeros((BP, CONV_W - 1, w_up.shape[2]), F32), p, tiles_p)

    tiles_s = dict(rows=(1, DB * TS), tm=DB * TS, G=DB, Tc=TS, Tw=TS)
    attend_s = lambda q, kt, vt: _moba_sample(q, kt, vt, cache_k[0], cache_v[0], page_table)
    ys, kts, vts, wkv_s, tshift_s, conv_s = _trunk(
        x_sample, attend_s, state_tshift[0], state_wkv[0], state_conv[0], p, tiles_s)

    heads_p = lambda t: jnp.transpose(t.reshape(BP, N_HEADS, HEAD_DIM, S), (0, 3, 1, 2))[None]
    heads_s = lambda t: jnp.transpose(t.reshape(N_HEADS, HEAD_DIM, DB, TS), (2, 3, 0, 1))[None]
    return (yp, ys, heads_p(ktp), heads_p(vtp), wkv_p[None], tshift_p[None], conv_p[None],
            heads_s(kts), heads_s(vts), wkv_s[None], tshift_s[None], conv_s[None])
```

```python
import functools

import jax
import jax.numpy as jnp
from jax import lax
from jax.experimental import pallas as pl
from jax.experimental.pallas import tpu as pltpu

F32 = jnp.float32
BF16 = jnp.bfloat16
HIGHEST = lax.Precision.HIGHEST
NT_DIMS = (((1,), (1,)), ((), ()))

HEAD_DIM = 64
N_HEADS = 8
WIDTH = N_HEADS * HEAD_DIM
MOBA_BLOCK = 256
MOBA_TOPK = 3
PAGE_SIZE = 128
PAGES_PER_BLOCK = MOBA_BLOCK // PAGE_SIZE
D_DECAY_LORA = 64
D_AAA_LORA = 64
D_GATE_LORA = 128
RWKV_PROJ = 3 * WIDTH + D_DECAY_LORA + D_AAA_LORA + D_GATE_LORA
CONV_W = 3
RMS_EPS = 1e-6
GN_EPS = 64e-5
NEG = -1e30
ATTN_SCALE = HEAD_DIM ** -0.5

LANES = 128
SUBLANES = 8
VMEM_LIMIT = 56 * 1024 * 1024


def _cparams(sem):
    return pltpu.CompilerParams(dimension_semantics=sem, vmem_limit_bytes=VMEM_LIMIT)


def _const_spec(shape):
    nd = len(shape)
    return pl.BlockSpec(shape, lambda *_: (0,) * nd, pipeline_mode=pl.Buffered(1))


def _sigmoid(x):
    return 1.0 / (1.0 + jnp.exp(-x))


def _rms(x, g):
    return x * lax.rsqrt(jnp.mean(x * x, axis=-1, keepdims=True) + RMS_EPS) * g


def _inproj_kernel(x_ref, g_ref, wq_ref, wkvt_ref, wpb_ref, wg_ref,
                   q_ref, kt_ref, vt_ref, pb_ref, sg_ref):
    xb = _rms(x_ref[...], g_ref[...]).astype(BF16)
    q_ref[...] = jnp.dot(xb, wq_ref[...], preferred_element_type=F32)
    kt_ref[...] = lax.dot_general(wkvt_ref[0:WIDTH, :], xb, NT_DIMS, preferred_element_type=F32)
    vt_ref[...] = lax.dot_general(wkvt_ref[WIDTH:2 * WIDTH, :], xb, NT_DIMS, preferred_element_type=F32)
    pb_ref[...] = jnp.dot(xb, wpb_ref[...], preferred_element_type=F32)
    sg_ref[...] = _sigmoid(jnp.dot(xb, wg_ref[...], preferred_element_type=F32))


def _inproj(x, ln_g, wq, wkvt, wpb, wg, tm):
    B, T, D = x.shape
    row = lambda w: pl.BlockSpec((None, tm, w), lambda b, t: (b, t, 0))
    col = pl.BlockSpec((None, WIDTH, tm), lambda b, t: (b, 0, t))
    return pl.pallas_call(
        _inproj_kernel,
        grid=(B, T // tm),
        in_specs=[row(D), _const_spec((1, D)), _const_spec(wq.shape), _const_spec(wkvt.shape),
                  _const_spec(wpb.shape), _const_spec(wg.shape)],
        out_specs=[row(WIDTH), col, col, row(RWKV_PROJ), row(2 * D)],
        out_shape=[jax.ShapeDtypeStruct((B, T, WIDTH), F32),
                   jax.ShapeDtypeStruct((B, WIDTH, T), F32),
                   jax.ShapeDtypeStruct((B, WIDTH, T), F32),
                   jax.ShapeDtypeStruct((B, T, RWKV_PROJ), F32),
                   jax.ShapeDtypeStruct((B, T, 2 * D), F32)],
        compiler_params=_cparams(("parallel", "parallel")),
        name="inproj",
    )(x, ln_g, wq, wkvt, wpb, wg)


def _moba_prompt_kernel(q_ref, kt_ref, vt_ref, o_ref, kp_ref, vtb_ref, s_ref, p_ref, *, nblk):
    S = q_ref.shape[0]
    kt = kt_ref[...]
    vtb_ref[...] = vt_ref[...].astype(BF16)
    lane = lax.broadcasted_iota(jnp.int32, (1, LANES), 1)
    rowd = lax.broadcasted_iota(jnp.int32, (LANES, 1), 0)
    keyblk = lax.broadcasted_iota(jnp.int32, (1, S), 1) // MOBA_BLOCK
    km = jnp.zeros((LANES, LANES), F32)
    first = rowd < HEAD_DIM
    for j in range(nblk):
        colj = jnp.sum(kt[:, j * MOBA_BLOCK:(j + 1) * MOBA_BLOCK], axis=1, keepdims=True) * (1.0 / MOBA_BLOCK)
        km = jnp.where(((lane == j) & jnp.logical_not(first)) | ((lane == HEAD_DIM + j) & first), colj, km)
    qrow = lax.broadcasted_iota(jnp.int32, (MOBA_BLOCK, 1), 0)
    kcol = lax.broadcasted_iota(jnp.int32, (1, MOBA_BLOCK), 1)
    for hh in range(2):
        in_r = (rowd >= HEAD_DIM * hh) & (rowd < HEAD_DIM * (hh + 1))
        ind = jnp.where(rowd - HEAD_DIM * (1 - hh) == keyblk, 1.0, 0.0)
        kp_ref[hh] = jnp.where(in_r, kt, ind).astype(BF16)
    for i in range(nblk):
        rows = slice(i * MOBA_BLOCK, (i + 1) * MOBA_BLOCK)
        qi = q_ref[rows, :]
        if i > 0:
            gate = jnp.dot(qi, km, precision=HIGHEST, preferred_element_type=F32)
        outs = []
        for hh in range(2):
            in_l = (lane >= HEAD_DIM * hh) & (lane < HEAD_DIM * (hh + 1))
            base = HEAD_DIM * (1 - hh)
            jl = lane - base
            is_bias_lane = (jl >= 0) & (jl < nblk)
            if i > 0:
                rank = jnp.zeros((MOBA_BLOCK, LANES), F32)
                for jp in range(i):
                    cj = gate[:, base + jp:base + jp + 1]
                    beats = (cj > gate) | ((cj == gate) & (jp < jl))
                    rank = rank + jnp.where(beats, 1.0, 0.0)
                keep = ((jl < i) & (rank < min(MOBA_TOPK, nblk - 1))) | (jl == i)
            else:
                keep = jl == 0
            bias = jnp.where(is_bias_lane & jnp.logical_not(keep), NEG, 0.0)
            qa = jnp.where(in_l, qi * ATTN_SCALE, bias).astype(BF16)
            sb, pb = s_ref.at[hh], p_ref.at[hh]
            nk = (i + 1) * MOBA_BLOCK
            mx = None
            for j in range(i + 1):
                keys = slice(j * MOBA_BLOCK, (j + 1) * MOBA_BLOCK)
                s = jnp.dot(qa, kp_ref[hh, :, keys], preferred_element_type=F32)
                if j == i:
                    s = jnp.where(kcol <= qrow, s, NEG)
                sb[:, keys] = s
                e = jnp.maximum(s[:, 0:LANES], s[:, LANES:2 * LANES])
                mx = e if mx is None else jnp.maximum(mx, e)
            m = jnp.max(mx, axis=-1, keepdims=True)
            ls = None
            for j in range(i + 1):
                keys = slice(j * MOBA_BLOCK, (j + 1) * MOBA_BLOCK)
                p = jnp.exp(sb[:, keys] - m)
                pb[:, keys] = p.astype(BF16)
                e = p[:, 0:LANES] + p[:, LANES:2 * LANES]
                ls = e if ls is None else ls + e
            l = jnp.sum(ls, axis=-1, keepdims=True)
            outs.append(lax.dot_general(pb[:, 0:nk], vtb_ref[:, 0:nk], NT_DIMS, preferred_element_type=F32) / l)
        o_ref[rows, :] = jnp.where(lane < HEAD_DIM, outs[0], outs[1])


def _moba_prompt(q, kt, vt):
    B, S, _ = q.shape
    nblk = S // MOBA_BLOCK
    qspec = pl.BlockSpec((None, S, LANES), lambda b, h: (b, 0, h))
    tspec = pl.BlockSpec((None, LANES, S), lambda b, h: (b, h, 0))
    return pl.pallas_call(
        functools.partial(_moba_prompt_kernel, nblk=nblk),
        grid=(B, WIDTH // LANES),
        in_specs=[qspec, tspec, tspec],
        out_specs=qspec,
        out_shape=jax.ShapeDtypeStruct((B, S, WIDTH), F32),
        scratch_shapes=[pltpu.VMEM((2, LANES, S), BF16), pltpu.VMEM((LANES, S), BF16),
                        pltpu.VMEM((2, MOBA_BLOCK, S), F32), pltpu.VMEM((2, MOBA_BLOCK, S), BF16)],
        compiler_params=_cparams(("parallel", "parallel")),
        name="moba_prompt",
    )(q, kt, vt)


def _kmean_accumulate(pages, km_ref, chunk):
    @pl.when(chunk == 0)
    def _():
        km_ref[...] = jnp.zeros(km_ref.shape, F32)

    lane = lax.broadcasted_iota(jnp.int32, (1, 1, LANES), 2)
    blocks_per_step = len(pages) // PAGES_PER_BLOCK
    for jj in range(blocks_per_step):
        tot = pages[PAGES_PER_BLOCK * jj][...]
        for r in range(1, PAGES_PER_BLOCK):
            tot = tot + pages[PAGES_PER_BLOCK * jj + r][...]
        colj = jnp.sum(tot, axis=-1, keepdims=True) * (1.0 / MOBA_BLOCK)
        km_ref[...] = jnp.where(lane == chunk * blocks_per_step + jj, colj, km_ref[...])


def _kmean_specs(n_pages, pages_per_step, step_of):
    spq = n_pages // pages_per_step

    def page_spec(r):
        def index(*ids):
            step, pt = step_of(*ids[:-1]), ids[-1]
            return (pt[step // spq, (step % spq) * pages_per_step + r], 0, 0, 0)
        return pl.BlockSpec((None, N_HEADS, HEAD_DIM, PAGE_SIZE), index)

    km_spec = pl.BlockSpec((None, N_HEADS, HEAD_DIM, LANES), lambda *ids: (step_of(*ids[:-1]) // spq, 0, 0, 0))
    return [page_spec(r) for r in range(pages_per_step)], km_spec


def _kmean_kernel(pt_ref, *refs):
    del pt_ref
    _kmean_accumulate(refs[:-1], refs[-1], pl.program_id(1))


def _kmean(ck, page_table, pages_per_step=32):
    DB, n_pages = page_table.shape
    pages_per_step = min(pages_per_step, n_pages)
    assert n_pages // PAGES_PER_BLOCK <= LANES and n_pages % pages_per_step == 0
    spq = n_pages // pages_per_step
    page_specs, km_spec = _kmean_specs(n_pages, pages_per_step, lambda b, c: b * spq + c)
    return pl.pallas_call(
        _kmean_kernel,
        grid_spec=pltpu.PrefetchScalarGridSpec(
            num_scalar_prefetch=1, grid=(DB, spq), in_specs=page_specs, out_specs=km_spec),
        out_shape=jax.ShapeDtypeStruct((DB, N_HEADS, HEAD_DIM, LANES), F32),
        compiler_params=_cparams(("parallel", "arbitrary")),
        name="moba_kmean",
    )(page_table, *([ck] * pages_per_step))


def _select_kernel(q_ref, km_ref, sel_ref, *, nblk):
    lane = lax.broadcasted_iota(jnp.int32, (1, LANES), 1)
    lanef = lane.astype(F32)
    for h in range(N_HEADS):
        gate = jnp.dot(q_ref[h], km_ref[h], precision=HIGHEST, preferred_element_type=F32)
        gate = jnp.where(lane < nblk, gate, -jnp.inf)
        out = jnp.zeros(gate.shape, F32)
        for r in range(MOBA_TOPK):
            m = jnp.max(gate, axis=-1, keepdims=True)
            idx = jnp.min(jnp.where(gate == m, lanef, float(LANES)), axis=-1, keepdims=True)
            out = jnp.where(lane == r, idx, out)
            gate = jnp.where(lanef == idx, -jnp.inf, gate)
        sel_ref[h] = out.astype(jnp.int32)


def _select(qh, km, nblk):
    DB, H, T, Dh = qh.shape
    return pl.pallas_call(
        functools.partial(_select_kernel, nblk=nblk),
        grid=(DB,),
        in_specs=[pl.BlockSpec((None, H, T, Dh), lambda b: (b, 0, 0, 0)),
                  pl.BlockSpec((None, H, Dh, LANES), lambda b: (b, 0, 0, 0))],
        out_specs=pl.BlockSpec((None, H, T, LANES), lambda b: (b, 0, 0, 0)),
        out_shape=jax.ShapeDtypeStruct((DB, H, T, LANES), jnp.int32),
        compiler_params=_cparams(("parallel",)),
        name="moba_select",
    )(qh, km)


def _sample_attn_kernel(sel_ref, pt_ref, qt_ref, knt_ref, vnt_ref, ck_hbm, cv_hbm, o_ref, kbuf, vbuf, sems,
                        *, T, n_sel):
    b, h = pl.program_id(0), pl.program_id(1)
    nh = pl.num_programs(1)
    step = b * nh + h
    nstep = pl.num_programs(0) * nh
    nkeys = n_sel * MOBA_BLOCK

    def copies(bb, hh, slot, lookup):
        out = []
        for t in range(T):
            for s in range(n_sel):
                blk = sel_ref[((bb * nh + hh) * T + t) * n_sel + s] if lookup else 0
                for r in range(PAGES_PER_BLOCK):
                    page = pt_ref[bb, blk * PAGES_PER_BLOCK + r] if lookup else 0
                    dst = pl.ds((s * PAGES_PER_BLOCK + r) * PAGE_SIZE, PAGE_SIZE)
                    out.append(pltpu.make_async_copy(ck_hbm.at[page, hh], kbuf.at[slot, t, :, dst], sems.at[slot]))
                    out.append(pltpu.make_async_copy(cv_hbm.at[page, hh], vbuf.at[slot, t, :, dst], sems.at[slot]))
        return out

    @pl.when(step == 0)
    def _():
        for cp in copies(b, h, 0, True):
            cp.start()

    @pl.when(step + 1 < nstep)
    def _():
        nxt = step + 1
        for cp in copies(nxt // nh, nxt % nh, nxt % 2, True):
            cp.start()

    slot = step % 2
    for cp in copies(0, 0, slot, False):
        cp.wait()

    qt = qt_ref[...]
    knt = knt_ref[...]
    vnt = vnt_ref[...]
    lane_t = lax.broadcasted_iota(jnp.int32, (1, T), 1)
    out = jnp.zeros((HEAD_DIM, LANES), F32)
    lane = lax.broadcasted_iota(jnp.int32, (1, LANES), 1)
    for t in range(T):
        qc = qt[:, t:t + 1]
        s_past = jnp.sum(kbuf[slot, t] * qc, axis=0, keepdims=True) * ATTN_SCALE
        s_new = jnp.sum(knt * qc, axis=0, keepdims=True) * ATTN_SCALE
        s_new = jnp.where(lane_t <= t, s_new, NEG)
        m = jnp.maximum(jnp.max(s_past, axis=-1, keepdims=True), jnp.max(s_new, axis=-1, keepdims=True))
        p_past = jnp.exp(s_past - m)
        p_new = jnp.exp(s_new - m)
        l = jnp.sum(p_past, axis=-1, keepdims=True) + jnp.sum(p_new, axis=-1, keepdims=True)
        o = (jnp.sum(vbuf[slot, t] * p_past, axis=-1, keepdims=True)
             + jnp.sum(vnt * p_new, axis=-1, keepdims=True)) / l
        out = jnp.where(lane == t, o, out)
    o_ref[...] = out


def _sample_attn(sel_flat, page_table, qt, knt, vnt, ck, cv, n_sel):
    DB, H, Dh, T = qt.shape
    small = pl.BlockSpec((None, None, Dh, T), lambda b, h, *_: (b, h, 0, 0))
    nkeys = n_sel * MOBA_BLOCK
    return pl.pallas_call(
        functools.partial(_sample_attn_kernel, T=T, n_sel=n_sel),
        grid_spec=pltpu.PrefetchScalarGridSpec(
            num_scalar_prefetch=2,
            grid=(DB, H),
            in_specs=[small, small, small, pl.BlockSpec(memory_space=pl.ANY), pl.BlockSpec(memory_space=pl.ANY)],
            out_specs=pl.BlockSpec((None, None, Dh, LANES), lambda b, h, *_: (b, h, 0, 0)),
            scratch_shapes=[pltpu.VMEM((2, T, Dh, nkeys), F32), pltpu.VMEM((2, T, Dh, nkeys), F32),
                            pltpu.SemaphoreType.DMA((2,))],
        ),
        out_shape=jax.ShapeDtypeStruct((DB, H, Dh, LANES), F32),
        compiler_params=_cparams(("arbitrary", "arbitrary")),
        name="moba_sample",
    )(sel_flat, page_table, qt, knt, vnt, ck, cv)


def _moba_sample(q, kt, vt, ck, cv, page_table, km):
    DB, n_pages = page_table.shape
    T = q.shape[1] // DB
    nblk = n_pages // PAGES_PER_BLOCK
    n_sel = min(MOBA_TOPK, nblk)
    assert n_pages % PAGES_PER_BLOCK == 0 and n_sel == MOBA_TOPK
    qh = jnp.transpose(q.reshape(DB, T, N_HEADS, HEAD_DIM), (0, 2, 1, 3))
    qt = jnp.transpose(qh, (0, 1, 3, 2))
    knt = jnp.transpose(kt.reshape(N_HEADS, HEAD_DIM, DB, T), (2, 0, 1, 3))
    vnt = jnp.transpose(vt.reshape(N_HEADS, HEAD_DIM, DB, T), (2, 0, 1, 3))
    sel = _select(qh, km, nblk)[..., :n_sel].reshape(-1)
    o = _sample_attn(sel, page_table, qt, knt, vnt, ck, cv, n_sel)
    return jnp.transpose(o[..., :T], (0, 3, 1, 2)).reshape(DB * T, WIDTH)


def _head_sum(x):
    z = x[:, 0:LANES]
    for p in range(1, x.shape[1] // LANES):
        z = z + x[:, p * LANES:(p + 1) * LANES]
    shift = N_HEADS
    while shift < LANES:
        z = z + pltpu.roll(z, shift, axis=1)
        shift *= 2
    return jnp.concatenate([z] * (x.shape[1] // LANES), axis=1)


def _rwkv_prep_kernel(pb_ref, prev_ref, mu_ref, w0_ref, w2p_ref, a0_ref, a2p_ref, g2_ref, kk_ref, ka_ref,
                      rk_ref, r_ref, w_ref, k_ref, v_ref, a_ref, b_ref, g_ref, bonus_ref, carry, *, transposed):
    c = pl.program_id(1)

    @pl.when(c == 0)
    def _():
        carry[...] = prev_ref[...]

    pb3 = pb_ref[...]
    G, Tc, P = pb3.shape
    tpos = lax.broadcasted_iota(jnp.int32, (1, Tc, 1), 1)
    prev3 = jnp.where(tpos == 0, carry[...], pltpu.roll(pb3, 1, axis=1))
    carry[...] = pb3[:, Tc - 1:Tc, :]
    pb = pb3.reshape(G * Tc, P)
    xm = pb + (prev3.reshape(G * Tc, P) - pb) * mu_ref[...]
    r = xm[:, 0:WIDTH]
    k = xm[:, WIDTH:2 * WIDTH]
    v = xm[:, 2 * WIDTH:3 * WIDTH]
    wa = xm[:, 3 * WIDTH:3 * WIDTH + D_DECAY_LORA + D_AAA_LORA]
    gd = xm[:, 3 * WIDTH + D_DECAY_LORA + D_AAA_LORA:]
    z = w0_ref[...] + jnp.dot(jnp.tanh(wa), w2p_ref[...], precision=HIGHEST, preferred_element_type=F32)
    softplus_negz = jnp.maximum(-z, 0.0) + jnp.log(1.0 + jnp.exp(-jnp.abs(z)))
    w = jnp.exp(-jnp.exp(-softplus_negz - 0.5))
    lr = _sigmoid(a0_ref[...] + jnp.dot(wa, a2p_ref[...], precision=HIGHEST, preferred_element_type=F32))
    g_ref[...] = jnp.dot(_sigmoid(gd), g2_ref[...], precision=HIGHEST, preferred_element_type=F32)
    kk = k * kk_ref[...]
    kk = kk / jnp.maximum(jnp.sqrt(_head_sum(kk * kk)), 1e-12)
    k2 = k * (1.0 + (lr - 1.0) * ka_ref[...])
    bonus_ref[...] = _head_sum(r * k2 * rk_ref[...]) * v
    outs = ((r_ref, r), (w_ref, w), (k_ref, k2), (v_ref, v), (a_ref, -kk), (b_ref, kk * lr))
    for ref, val in outs:
        ref[...] = val.T if transposed else val


def _rwkv_prep(pb, prev_row, consts, G, Tc, transposed):
    B, T, P = pb.shape
    nt = T // Tc
    assert G == 1 or nt == 1
    nat = jax.ShapeDtypeStruct((B * T, WIDTH), F32)
    nat_spec = pl.BlockSpec((G * Tc, WIDTH), lambda b, c: (b * nt + c, 0))
    if transposed:
        assert G == 1
        vec, vec_spec = jax.ShapeDtypeStruct((B, WIDTH, T), F32), pl.BlockSpec((None, WIDTH, Tc), lambda b, c: (b, 0, c))
    else:
        vec, vec_spec = nat, nat_spec
    return pl.pallas_call(
        functools.partial(_rwkv_prep_kernel, transposed=transposed),
        grid=(B // G, nt),
        in_specs=[pl.BlockSpec((G, Tc, P), lambda b, c: (b, c, 0)),
                  pl.BlockSpec((G, 1, P), lambda b, c: (b, 0, 0))] + [_const_spec(a.shape) for a in consts],
        out_specs=[vec_spec] * 6 + [nat_spec] * 2,
        out_shape=[vec] * 6 + [nat] * 2,
        scratch_shapes=[pltpu.VMEM((G, 1, P), F32)],
        compiler_params=_cparams(("parallel", "arbitrary")),
        name="rwkv_prep",
    )(pb, prev_row, *consts)


LANE_CHUNK = 128


def _to_lanes_kernel(x_ref, o_ref, *, tw):
    nb = x_ref.shape[0]
    for j in range(HEAD_DIM):
        m = x_ref[:, j * N_HEADS:(j + 1) * N_HEADS, :].reshape(nb * N_HEADS, LANE_CHUNK)
        mt = m.T
        for q in range(LANE_CHUNK // tw):
            o_ref[q, j] = mt[q * tw:(q + 1) * tw, :]


def _to_lanes(xt, tw):
    B, W, T = xt.shape
    L = B * N_HEADS
    assert L <= LANES and T % LANE_CHUNK == 0 and LANE_CHUNK % tw == 0
    return pl.pallas_call(
        functools.partial(_to_lanes_kernel, tw=tw),
        grid=(T // LANE_CHUNK,),
        in_specs=[pl.BlockSpec((B, W, LANE_CHUNK), lambda c: (0, 0, c))],
        out_specs=pl.BlockSpec((LANE_CHUNK // tw, HEAD_DIM, tw, L), lambda c: (c, 0, 0, 0)),
        out_shape=jax.ShapeDtypeStruct((T // tw, HEAD_DIM, tw, L), F32),
        compiler_params=_cparams(("parallel",)),
        name="to_lanes",
    )(xt)


def _from_lanes_kernel(y_ref, o_ref):
    nb = o_ref.shape[0]
    for i in range(HEAD_DIM):
        m = y_ref[pl.ds(i, LANE_CHUNK, stride=HEAD_DIM), :]
        o_ref[:, i * N_HEADS:(i + 1) * N_HEADS, :] = m.T.reshape(nb, N_HEADS, LANE_CHUNK)


def _from_lanes(y2d, B):
    TN, L = y2d.shape
    T = TN // HEAD_DIM
    assert L == B * N_HEADS <= LANES and T % LANE_CHUNK == 0
    return pl.pallas_call(
        _from_lanes_kernel,
        grid=(T // LANE_CHUNK,),
        in_specs=[pl.BlockSpec((LANE_CHUNK * HEAD_DIM, L), lambda c: (c, 0))],
        out_specs=pl.BlockSpec((B, WIDTH, LANE_CHUNK), lambda c: (0, 0, c)),
        out_shape=jax.ShapeDtypeStruct((B, WIDTH, T), F32),
        compiler_params=_cparams(("parallel",)),
        name="from_lanes",
    )(y2d)


WKV_UNROLL = 32


def _wkv_kernel(*refs, side_pages, side_steps, side_spq):
    if side_pages:
        refs = refs[1:]
    r_ref, w_ref, k_ref, v_ref, a_ref, b_ref, s0_ref = refs[:7]
    pages = refs[7:7 + side_pages]
    y_ref, sfin_ref = refs[7 + side_pages:9 + side_pages]
    S = refs[-1]
    c = pl.program_id(1)
    N, Tw = r_ref.shape[0], r_ref.shape[1]

    @pl.when(c == 0)
    def _():
        S[...] = s0_ref[...]

    if side_pages:
        @pl.when(c < side_steps)
        def _():
            _kmean_accumulate(pages, refs[9 + side_pages], c % side_spq)

    def token(t, carry):
        row = lambda ref, j: ref[j, pl.ds(t, 1), :]

        def sa_body(jb, sa):
            for jj in range(WKV_UNROLL):
                j = jb * WKV_UNROLL + jj
                sa = sa + S[j] * row(a_ref, j)
            return sa

        sa = lax.fori_loop(0, N // WKV_UNROLL, sa_body, jnp.zeros(S.shape[1:], F32))
        vt = v_ref[pl.ds(t, N, stride=Tw), :]

        def up_body(jb, y):
            for jj in range(WKV_UNROLL):
                j = jb * WKV_UNROLL + jj
                s = S[j] * row(w_ref, j) + sa * row(b_ref, j) + vt * row(k_ref, j)
                S[j] = s
                y = y + s * row(r_ref, j)
            return y

        y_ref[t] = lax.fori_loop(0, N // WKV_UNROLL, up_body, jnp.zeros(S.shape[1:], F32))
        return carry

    lax.fori_loop(0, Tw, token, 0)

    @pl.when(c == pl.num_programs(1) - 1)
    def _():
        sfin_ref[...] = S[...]


def _side_pages_per_step(page_table, n_steps):
    DB, n_pages = page_table.shape
    for pps in range(PAGES_PER_BLOCK, min(n_pages, 32) + 1, PAGES_PER_BLOCK):
        if n_pages % pps == 0 and DB * (n_pages // pps) <= n_steps:
            return pps
    return 0


def _wkv(r, w, k, v, a, b, s0, side=None):
    nc, N, Tw, L = r.shape
    lb = min(LANES, L)
    vec = pl.BlockSpec((None, N, Tw, lb), lambda g, c, *_: (c, 0, 0, g))
    st = pl.BlockSpec((N, N, lb), lambda g, c, *_: (0, 0, g))
    in_specs = [vec, vec, vec, pl.BlockSpec((N * Tw, lb), lambda g, c, *_: (c, g)), vec, vec, st]
    out_specs = [pl.BlockSpec((Tw, N, lb), lambda g, c, *_: (c, 0, g)), st]
    out_shape = [jax.ShapeDtypeStruct((nc * Tw, N, L), F32), jax.ShapeDtypeStruct((N, N, L), F32)]
    operands = [r, w, k, v.reshape(nc * N * Tw, L), a, b, s0]
    prefetch, pps, side_steps, spq = [], 0, 0, 1
    if side is not None:
        ck, page_table = side
        assert L == lb
        pps = _side_pages_per_step(page_table, nc)
        spq = page_table.shape[1] // pps
        side_steps = page_table.shape[0] * spq
        page_specs, km_spec = _kmean_specs(page_table.shape[1], pps, lambda g, c: jnp.minimum(c, side_steps - 1))
        in_specs += page_specs
        out_specs.append(km_spec)
        out_shape.append(jax.ShapeDtypeStruct((page_table.shape[0], N_HEADS, HEAD_DIM, LANES), F32))
        operands += [ck] * pps
        prefetch = [page_table]
    return pl.pallas_call(
        functools.partial(_wkv_kernel, side_pages=pps, side_steps=side_steps, side_spq=spq),
        grid_spec=pltpu.PrefetchScalarGridSpec(
            num_scalar_prefetch=len(prefetch), grid=(L // lb, nc), in_specs=in_specs, out_specs=out_specs,
            scratch_shapes=[pltpu.VMEM((N, N, lb), F32)]),
        out_shape=out_shape,
        compiler_params=_cparams(("parallel", "arbitrary")),
        name="rwkv_wkv",
    )(*prefetch, *operands)


def _rwkv(pb, prev_row, wkv0, consts, G, Tc, Tw, side=None):
    B, T, _ = pb.shape
    L = B * N_HEADS
    big = T % LANE_CHUNK == 0
    if side is not None and not (big and _side_pages_per_step(side[1], T // Tw)):
        side = None
    r, w, k, v, a, b, g, bonus = _rwkv_prep(pb, prev_row[:, None, :], consts, G, Tc, big)
    if big:
        coef = [_to_lanes(x, Tw) for x in (r, w, k, v, a, b)]
    else:
        assert Tw == T
        coef = [jnp.transpose(x.reshape(B, T, HEAD_DIM, N_HEADS), (2, 1, 0, 3)).reshape(1, HEAD_DIM, T, L)
                for x in (r, w, k, v, a, b)]
    s0 = jnp.transpose(wkv0, (3, 2, 0, 1)).reshape(HEAD_DIM, HEAD_DIM, L)
    y, sfin, *side_out = _wkv(*coef, s0, side)
    if big:
        y = _from_lanes(y.reshape(T * HEAD_DIM, L), B)
    else:
        y = jnp.transpose(y.reshape(T, HEAD_DIM, B, N_HEADS), (2, 0, 1, 3)).reshape(B * T, WIDTH)
    sfin = jnp.transpose(sfin.reshape(HEAD_DIM, HEAD_DIM, B, N_HEADS), (2, 3, 1, 0))
    return y, g, bonus, sfin, (side_out[0] if side_out else None)


def _merge_kernel(x_ref, at_ref, y_ref, bonus_ref, g_ref, sg_ref, lng_ref, lnb_ref,
                  wa_ref, wb_ref, wo_ref, h_ref, *, y_transposed):
    D = x_ref.shape[1]
    y = y_ref[...].T if y_transposed else y_ref[...]
    d = y - _head_sum(y) * (1.0 / HEAD_DIM)
    var = _head_sum(d * d) * (1.0 / HEAD_DIM)
    yn = d * lax.rsqrt(var + GN_EPS) * lng_ref[...] + lnb_ref[...]
    ob = (yn + bonus_ref[...]) * g_ref[...]
    y_a = jnp.dot(at_ref[...].astype(BF16), wa_ref[...], preferred_element_type=F32)
    y_b = jnp.dot(ob.astype(BF16), wb_ref[...], preferred_element_type=F32)
    mix = sg_ref[:, 0:D] * y_a + sg_ref[:, D:2 * D] * y_b
    h_ref[...] = x_ref[...] + jnp.dot(mix.astype(BF16), wo_ref[...], preferred_element_type=F32)


def _merge(x, attn, y, bonus, g, sg, lng, lnb, wa, wb, wo, tm, y_transposed):
    B, T, D = x.shape
    row = lambda w: pl.BlockSpec((None, tm, w), lambda b, t: (b, t, 0))
    yspec = pl.BlockSpec((None, WIDTH, tm), lambda b, t: (b, 0, t)) if y_transposed else row(WIDTH)
    consts = (lng, lnb, wa, wb, wo)
    return pl.pallas_call(
        functools.partial(_merge_kernel, y_transposed=y_transposed),
        grid=(B, T // tm),
        in_specs=[row(D), row(WIDTH), yspec, row(WIDTH), row(WIDTH), row(2 * D)]
                 + [_const_spec(a.shape) for a in consts],
        out_specs=row(D),
        out_shape=jax.ShapeDtypeStruct((B, T, D), F32),
        compiler_params=_cparams(("parallel", "parallel")),
        name="merge",
    )(x, attn, y, bonus, g, sg, *consts)


FFN_CHUNK = 256


def _ffn_kernel(h_ref, cp_ref, ln2_ref, wup_ref, cw_ref, cb_ref, wdn_ref, lnf_ref, y_ref, cs_ref, carry, act_ref,
                *, dff):
    c = pl.program_id(1)

    @pl.when(c == 0)
    def _():
        carry[...] = cp_ref[...]

    h3 = h_ref[...]
    G, Tc, D = h3.shape
    h = h3.reshape(G * Tc, D)
    xb = _rms(h, ln2_ref[...]).astype(BF16)
    thead = lax.broadcasted_iota(jnp.int32, (1, SUBLANES, 1), 1)

    def conv(cols):
        u = jnp.dot(xb, wup_ref[:, cols], preferred_element_type=F32).reshape(G, Tc, FFN_CHUNK)
        w0, w1, w2, cb = cw_ref[0:1, cols], cw_ref[1:2, cols], cw_ref[2:3, cols], cb_ref[:, cols]
        r1 = pltpu.roll(u, 1, axis=1)
        r2 = pltpu.roll(u, 2, axis=1)
        p0 = carry[:, 0:1, cols]
        p1 = carry[:, 1:2, cols]
        h1 = jnp.where(thead == 0, p1, r1[:, 0:SUBLANES, :])
        h2 = jnp.where(thead == 0, p0, jnp.where(thead == 1, p1, r2[:, 0:SUBLANES, :]))
        cv = cb + h2 * w0 + h1 * w1 + u[:, 0:SUBLANES, :] * w2
        if Tc > SUBLANES:
            rest = slice(SUBLANES, Tc)
            cv = jnp.concatenate([cv, cb + r2[:, rest, :] * w0 + r1[:, rest, :] * w1 + u[:, rest, :] * w2], axis=1)
        carry[:, :, cols] = u[:, Tc - (CONV_W - 1):Tc, :]
        return cv.reshape(G * Tc, FFN_CHUNK)

    for n in range(dff // FFN_CHUNK):
        val = conv(slice(n * FFN_CHUNK, (n + 1) * FFN_CHUNK))
        gt = conv(slice(dff + n * FFN_CHUNK, dff + (n + 1) * FFN_CHUNK))
        act_ref[:, n * FFN_CHUNK:(n + 1) * FFN_CHUNK] = (gt * _sigmoid(gt) * val).astype(BF16)
    cs_ref[...] = carry[...]
    out = h + jnp.dot(act_ref[...], wdn_ref[...], preferred_element_type=F32)
    y_ref[...] = _rms(out, lnf_ref[...]).reshape(G, Tc, D)


def _ffn(h, conv_prev, ln2, wup, cw, cb, wdn, lnf, G, Tc):
    B, T, D = h.shape
    dff = wdn.shape[0]
    assert dff % FFN_CHUNK == 0 and Tc >= CONV_W - 1 and (G == 1 or Tc == T)
    consts_a = (ln2, wup, cw, cb, wdn, lnf)
    tile = pl.BlockSpec((G, Tc, D), lambda b, c: (b, c, 0))
    cst = pl.BlockSpec((G, CONV_W - 1, 2 * dff), lambda b, c: (b, 0, 0))
    return pl.pallas_call(
        functools.partial(_ffn_kernel, dff=dff),
        grid=(B // G, T // Tc),
        in_specs=[tile, cst] + [_const_spec(a.shape) for a in consts_a],
        out_specs=[tile, cst],
        out_shape=[jax.ShapeDtypeStruct((B, T, D), F32), jax.ShapeDtypeStruct((B, CONV_W - 1, 2 * dff), F32)],
        scratch_shapes=[pltpu.VMEM((G, CONV_W - 1, 2 * dff), F32), pltpu.VMEM((G * Tc, dff), BF16)],
        compiler_params=_cparams(("parallel", "arbitrary")),
        name="convffn",
    )(h, conv_prev, *consts_a)


def _trunk(x, attend, shift_prev, wkv0, conv_prev, p, tiles):
    B, T, D = x.shape
    Bv, Tv = tiles["rows"]
    q, kt, vt, pb, sg = _inproj(x.reshape(Bv, Tv, D), p["ln1_g"], p["wq"], p["wkvt"], p["wpb"], p["wg"], tiles["tm"])
    attn = attend(q, kt, vt).reshape(Bv, Tv, WIDTH)
    pb = pb.reshape(B, T, RWKV_PROJ)
    y, g, bonus, sfin, side_out = _rwkv(pb, shift_prev[:, p["shift_cols"]], wkv0, p["rwkv_consts"],
                                        tiles["G"], tiles["Tc"], tiles["Tw"], tiles.get("side"))
    y_transposed = y.ndim == 3
    if not y_transposed:
        y = y.reshape(Bv, Tv, WIDTH)
    h = _merge(x.reshape(Bv, Tv, D), attn, y, bonus.reshape(Bv, Tv, WIDTH), g.reshape(Bv, Tv, WIDTH), sg,
               p["lnx_g"], p["lnx_b"], p["w_br_a"], p["w_br_b"], p["w_o"], tiles["tm"], y_transposed)
    yout, conv_new = _ffn(h.reshape(B, T, D), conv_prev, p["ln2_g"], p["w_up"], p["conv_w"], p["conv_b"],
                          p["w_down"], p["lnf_g"], tiles["G"], tiles["Tc"])
    return yout, kt, vt, sfin, pb[:, -1, :][:, p["unshift_cols"]], conv_new, side_out


def kernel(x_prompt, x_sample, cache_k, cache_v, page_table, state_wkv, state_tshift, state_conv, ln1_g, w_in, w_br_a, mu, w0, w2, a0, a2, g2, k_k, k_a, r_k, lnx_g, lnx_b, w_br_b, w_o, ln2_g, w_up, conv_w, conv_b, w_down, lnf_g):
    assert w_in.shape[0] == 1, "single-layer trunk"
    BP, S, D = x_prompt.shape
    DB, TS, _ = x_sample.shape
    zrows = jnp.zeros((D_AAA_LORA, WIDTH), F32)
    lane = jnp.arange(WIDTH)
    perm = (lane % N_HEADS) * HEAD_DIM + lane // N_HEADS
    inv = jnp.argsort(perm)
    tail = jnp.arange(3 * WIDTH, RWKV_PROJ)
    shift_cols = jnp.concatenate([perm, WIDTH + perm, 2 * WIDTH + perm, tail])
    unshift_cols = jnp.concatenate([inv, WIDTH + inv, 2 * WIDTH + inv, tail])
    wpb = w_in[0, :, 3 * WIDTH:3 * WIDTH + RWKV_PROJ][:, shift_cols]
    p = dict(
        ln1_g=ln1_g, ln2_g=ln2_g, lnf_g=lnf_g[None, :], lnx_g=lnx_g[:, perm], lnx_b=lnx_b[:, perm],
        wq=w_in[0, :, 0:WIDTH].astype(BF16),
        wkvt=jnp.transpose(w_in[0, :, WIDTH:3 * WIDTH]).astype(BF16),
        wpb=wpb.astype(BF16),
        wg=w_in[0, :, 3 * WIDTH + RWKV_PROJ:].astype(BF16),
        w_br_a=w_br_a[0].astype(BF16), w_br_b=w_br_b[0][perm, :].astype(BF16), w_o=w_o[0].astype(BF16),
        w_up=w_up[0].astype(BF16), w_down=w_down[0].astype(BF16), conv_w=conv_w[0], conv_b=conv_b,
        shift_cols=shift_cols, unshift_cols=unshift_cols,
    )
    p["rwkv_consts"] = (mu[:, shift_cols], w0[:, perm], jnp.concatenate([w2[0], zrows], axis=0)[:, perm],
                        a0[:, perm], jnp.concatenate([zrows, a2[0]], axis=0)[:, perm], g2[0][:, perm],
                        k_k[:, perm], k_a[:, perm], r_k.reshape(1, WIDTH)[:, perm])

    ck = jnp.transpose(cache_k[0], (0, 2, 3, 1))
    cv = jnp.transpose(cache_v[0], (0, 2, 3, 1))

    tiles_p = dict(rows=(BP, S), tm=min(512, S), G=1, Tc=min(512, S), Tw=min(16, S), side=(ck, page_table))
    yp, ktp, vtp, wkv_p, tshift_p, conv_p, km = _trunk(
        x_prompt, lambda q, kt, vt: _moba_prompt(q, kt, vt),
        jnp.zeros((BP, RWKV_PROJ), F32), jnp.zeros((BP, N_HEADS, HEAD_DIM, HEAD_DIM), F32),
        jnp.zeros((BP, CONV_W - 1, w_up.shape[2]), F32), p, tiles_p)
    if km is None:
        km = _kmean(ck, page_table)

    tiles_s = dict(rows=(1, DB * TS), tm=DB * TS, G=DB, Tc=TS, Tw=TS)
    attend_s = lambda q, kt, vt: _moba_sample(q, kt, vt, ck, cv, page_table, km)
    ys, kts, vts, wkv_s, tshift_s, conv_s, _ = _trunk(
        x_sample, attend_s, state_tshift[0], state_wkv[0], state_conv[0], p, tiles_s)

    heads_p = lambda t: jnp.transpose(t.reshape(BP, N_HEADS, HEAD_DIM, S), (0, 3, 1, 2))[None]
    heads_s = lambda t: jnp.transpose(t.reshape(N_HEADS, HEAD_DIM, DB, TS), (2, 3, 0, 1))[None]
    return (yp, ys, heads_p(ktp), heads_p(vtp), wkv_p[None], tshift_p[None], conv_p[None],
            heads_s(kts), heads_s(vts), wkv_s[None], tshift_s[None], conv_s[None])
```

```python
import functools

import jax
import jax.numpy as jnp
from jax import lax
from jax.experimental import pallas as pl
from jax.experimental.pallas import tpu as pltpu

F32 = jnp.float32
BF16 = jnp.bfloat16
HIGHEST = lax.Precision.HIGHEST
NT_DIMS = (((1,), (1,)), ((), ()))

HEAD_DIM = 64
N_HEADS = 8
WIDTH = N_HEADS * HEAD_DIM
MOBA_BLOCK = 256
MOBA_TOPK = 3
PAGE_SIZE = 128
PAGES_PER_BLOCK = MOBA_BLOCK // PAGE_SIZE
D_DECAY_LORA = 64
D_AAA_LORA = 64
D_GATE_LORA = 128
RWKV_PROJ = 3 * WIDTH + D_DECAY_LORA + D_AAA_LORA + D_GATE_LORA
CONV_W = 3
RMS_EPS = 1e-6
GN_EPS = 64e-5
NEG = -1e30
ATTN_SCALE = HEAD_DIM ** -0.5

LANES = 128
SUBLANES = 8
VMEM_LIMIT = 56 * 1024 * 1024


def _cparams(sem):
    return pltpu.CompilerParams(dimension_semantics=sem, vmem_limit_bytes=VMEM_LIMIT)


def _const_spec(shape):
    nd = len(shape)
    return pl.BlockSpec(shape, lambda *_: (0,) * nd, pipeline_mode=pl.Buffered(1))


def _sigmoid(x):
    return 1.0 / (1.0 + jnp.exp(-x))


def _rms(x, g):
    return x * lax.rsqrt(jnp.mean(x * x, axis=-1, keepdims=True) + RMS_EPS) * g


def _inproj_kernel(x_ref, g_ref, wq_ref, wkvt_ref, wpb_ref, wg_ref,
                   q_ref, kt_ref, vt_ref, pb_ref, sg_ref):
    xb = _rms(x_ref[...], g_ref[...]).astype(BF16)
    q_ref[...] = jnp.dot(xb, wq_ref[...], preferred_element_type=F32)
    kt_ref[...] = lax.dot_general(wkvt_ref[0:WIDTH, :], xb, NT_DIMS, preferred_element_type=F32)
    vt_ref[...] = lax.dot_general(wkvt_ref[WIDTH:2 * WIDTH, :], xb, NT_DIMS, preferred_element_type=F32)
    pb_ref[...] = jnp.dot(xb, wpb_ref[...], preferred_element_type=F32)
    sg_ref[...] = _sigmoid(jnp.dot(xb, wg_ref[...], preferred_element_type=F32)).astype(sg_ref.dtype)


def _inproj(x, ln_g, wq, wkvt, wpb, wg, tm):
    B, T, D = x.shape
    row = lambda w: pl.BlockSpec((None, tm, w), lambda b, t: (b, t, 0))
    col = pl.BlockSpec((None, WIDTH, tm), lambda b, t: (b, 0, t))
    return pl.pallas_call(
        _inproj_kernel,
        grid=(B, T // tm),
        in_specs=[row(D), _const_spec((1, D)), _const_spec(wq.shape), _const_spec(wkvt.shape),
                  _const_spec(wpb.shape), _const_spec(wg.shape)],
        out_specs=[row(WIDTH), col, col, row(RWKV_PROJ), row(2 * D)],
        out_shape=[jax.ShapeDtypeStruct((B, T, WIDTH), F32),
                   jax.ShapeDtypeStruct((B, WIDTH, T), F32),
                   jax.ShapeDtypeStruct((B, WIDTH, T), F32),
                   jax.ShapeDtypeStruct((B, T, RWKV_PROJ), F32),
                   jax.ShapeDtypeStruct((B, T, 2 * D), BF16)],
        compiler_params=_cparams(("parallel", "parallel")),
        name="inproj",
    )(x, ln_g, wq, wkvt, wpb, wg)


def _moba_prompt_kernel(q_ref, kt_ref, vt_ref, o_ref, kp_ref, vtb_ref, s_ref, p_ref, *, nblk):
    S = q_ref.shape[0]
    kt = kt_ref[...]
    vtb_ref[...] = vt_ref[...].astype(BF16)
    lane = lax.broadcasted_iota(jnp.int32, (1, LANES), 1)
    rowd = lax.broadcasted_iota(jnp.int32, (LANES, 1), 0)
    keyblk = lax.broadcasted_iota(jnp.int32, (1, S), 1) // MOBA_BLOCK
    km = jnp.zeros((LANES, LANES), F32)
    first = rowd < HEAD_DIM
    for j in range(nblk):
        colj = jnp.sum(kt[:, j * MOBA_BLOCK:(j + 1) * MOBA_BLOCK], axis=1, keepdims=True) * (1.0 / MOBA_BLOCK)
        km = jnp.where(((lane == j) & jnp.logical_not(first)) | ((lane == HEAD_DIM + j) & first), colj, km)
    qrow = lax.broadcasted_iota(jnp.int32, (MOBA_BLOCK, 1), 0)
    kcol = lax.broadcasted_iota(jnp.int32, (1, MOBA_BLOCK), 1)
    for hh in range(2):
        in_r = (rowd >= HEAD_DIM * hh) & (rowd < HEAD_DIM * (hh + 1))
        ind = jnp.where(rowd - HEAD_DIM * (1 - hh) == keyblk, 1.0, 0.0)
        kp_ref[hh] = jnp.where(in_r, kt, ind).astype(BF16)
    for i in range(nblk):
        rows = slice(i * MOBA_BLOCK, (i + 1) * MOBA_BLOCK)
        qi = q_ref[rows, :]
        if i > 0:
            gate = jnp.dot(qi, km, precision=HIGHEST, preferred_element_type=F32)
        outs = []
        for hh in range(2):
            in_l = (lane >= HEAD_DIM * hh) & (lane < HEAD_DIM * (hh + 1))
            base = HEAD_DIM * (1 - hh)
            jl = lane - base
            is_bias_lane = (jl >= 0) & (jl < nblk)
            if i > 0:
                rank = jnp.zeros((MOBA_BLOCK, LANES), F32)
                for jp in range(i):
                    cj = gate[:, base + jp:base + jp + 1]
                    beats = (cj > gate) | ((cj == gate) & (jp < jl))
                    rank = rank + jnp.where(beats, 1.0, 0.0)
                keep = ((jl < i) & (rank < min(MOBA_TOPK, nblk - 1))) | (jl == i)
            else:
                keep = jl == 0
            bias = jnp.where(is_bias_lane & jnp.logical_not(keep), NEG, 0.0)
            qa = jnp.where(in_l, qi * ATTN_SCALE, bias).astype(BF16)
            sb, pb = s_ref.at[hh], p_ref.at[hh]
            nk = (i + 1) * MOBA_BLOCK
            mx = None
            for j in range(i + 1):
                keys = slice(j * MOBA_BLOCK, (j + 1) * MOBA_BLOCK)
                s = jnp.dot(qa, kp_ref[hh, :, keys], preferred_element_type=F32)
                if j == i:
                    s = jnp.where(kcol <= qrow, s, NEG)
                sb[:, keys] = s
                e = jnp.maximum(s[:, 0:LANES], s[:, LANES:2 * LANES])
                mx = e if mx is None else jnp.maximum(mx, e)
            m = jnp.max(mx, axis=-1, keepdims=True)
            ls = None
            for j in range(i + 1):
                keys = slice(j * MOBA_BLOCK, (j + 1) * MOBA_BLOCK)
                p = jnp.exp(sb[:, keys] - m)
                pb[:, keys] = p.astype(BF16)
                e = p[:, 0:LANES] + p[:, LANES:2 * LANES]
                ls = e if ls is None else ls + e
            l = jnp.sum(ls, axis=-1, keepdims=True)
            outs.append(lax.dot_general(pb[:, 0:nk], vtb_ref[:, 0:nk], NT_DIMS, preferred_element_type=F32) / l)
        o_ref[rows, :] = jnp.where(lane < HEAD_DIM, outs[0], outs[1]).astype(o_ref.dtype)


def _moba_prompt(q, kt, vt):
    B, S, _ = q.shape
    nblk = S // MOBA_BLOCK
    qspec = pl.BlockSpec((None, S, LANES), lambda b, h: (b, 0, h))
    tspec = pl.BlockSpec((None, LANES, S), lambda b, h: (b, h, 0))
    return pl.pallas_call(
        functools.partial(_moba_prompt_kernel, nblk=nblk),
        grid=(B, WIDTH // LANES),
        in_specs=[qspec, tspec, tspec],
        out_specs=qspec,
        out_shape=jax.ShapeDtypeStruct((B, S, WIDTH), BF16),
        scratch_shapes=[pltpu.VMEM((2, LANES, S), BF16), pltpu.VMEM((LANES, S), BF16),
                        pltpu.VMEM((2, MOBA_BLOCK, S), F32), pltpu.VMEM((2, MOBA_BLOCK, S), BF16)],
        compiler_params=_cparams(("parallel", "parallel")),
        name="moba_prompt",
    )(q, kt, vt)


def _kmean_accumulate(pages, km_ref, chunk):
    @pl.when(chunk == 0)
    def _():
        km_ref[...] = jnp.zeros(km_ref.shape, F32)

    lane = lax.broadcasted_iota(jnp.int32, (1, 1, LANES), 2)
    blocks_per_step = len(pages) // PAGES_PER_BLOCK
    for jj in range(blocks_per_step):
        tot = pages[PAGES_PER_BLOCK * jj][...]
        for r in range(1, PAGES_PER_BLOCK):
            tot = tot + pages[PAGES_PER_BLOCK * jj + r][...]
        colj = jnp.sum(tot, axis=-1, keepdims=True) * (1.0 / MOBA_BLOCK)
        km_ref[...] = jnp.where(lane == chunk * blocks_per_step + jj, colj, km_ref[...])


def _kmean_specs(n_pages, pages_per_step, step_of):
    spq = n_pages // pages_per_step

    def page_spec(r):
        def index(*ids):
            step, pt = step_of(*ids[:-1]), ids[-1]
            return (pt[step // spq, (step % spq) * pages_per_step + r], 0, 0, 0)
        return pl.BlockSpec((None, N_HEADS, HEAD_DIM, PAGE_SIZE), index)

    km_spec = pl.BlockSpec((None, N_HEADS, HEAD_DIM, LANES), lambda *ids: (step_of(*ids[:-1]) // spq, 0, 0, 0))
    return [page_spec(r) for r in range(pages_per_step)], km_spec


def _kmean_kernel(pt_ref, *refs):
    del pt_ref
    _kmean_accumulate(refs[:-1], refs[-1], pl.program_id(1))


def _kmean(ck, page_table, pages_per_step=32):
    DB, n_pages = page_table.shape
    pages_per_step = min(pages_per_step, n_pages)
    assert n_pages // PAGES_PER_BLOCK <= LANES and n_pages % pages_per_step == 0
    spq = n_pages // pages_per_step
    page_specs, km_spec = _kmean_specs(n_pages, pages_per_step, lambda b, c: b * spq + c)
    return pl.pallas_call(
        _kmean_kernel,
        grid_spec=pltpu.PrefetchScalarGridSpec(
            num_scalar_prefetch=1, grid=(DB, spq), in_specs=page_specs, out_specs=km_spec),
        out_shape=jax.ShapeDtypeStruct((DB, N_HEADS, HEAD_DIM, LANES), F32),
        compiler_params=_cparams(("parallel", "arbitrary")),
        name="moba_kmean",
    )(page_table, *([ck] * pages_per_step))


def _select_kernel(q_ref, km_ref, sel_ref, *, nblk):
    lane = lax.broadcasted_iota(jnp.int32, (1, LANES), 1)
    lanef = lane.astype(F32)
    for h in range(N_HEADS):
        gate = jnp.dot(q_ref[h], km_ref[h], precision=HIGHEST, preferred_element_type=F32)
        gate = jnp.where(lane < nblk, gate, -jnp.inf)
        out = jnp.zeros(gate.shape, F32)
        for r in range(MOBA_TOPK):
            m = jnp.max(gate, axis=-1, keepdims=True)
            idx = jnp.min(jnp.where(gate == m, lanef, float(LANES)), axis=-1, keepdims=True)
            out = jnp.where(lane == r, idx, out)
            gate = jnp.where(lanef == idx, -jnp.inf, gate)
        sel_ref[h] = out.astype(jnp.int32)


def _select(qh, km, nblk):
    DB, H, T, Dh = qh.shape
    return pl.pallas_call(
        functools.partial(_select_kernel, nblk=nblk),
        grid=(DB,),
        in_specs=[pl.BlockSpec((None, H, T, Dh), lambda b: (b, 0, 0, 0)),
                  pl.BlockSpec((None, H, Dh, LANES), lambda b: (b, 0, 0, 0))],
        out_specs=pl.BlockSpec((None, H, T, LANES), lambda b: (b, 0, 0, 0)),
        out_shape=jax.ShapeDtypeStruct((DB, H, T, LANES), jnp.int32),
        compiler_params=_cparams(("parallel",)),
        name="moba_select",
    )(qh, km)


SAMPLE_HEADS_PER_STEP = 1


def _sample_attn_kernel(sel_ref, pt_ref, qt_ref, knt_ref, vnt_ref, ck_hbm, cv_hbm, o_ref, kbuf, vbuf, sems,
                        *, T, n_sel, n_heads):
    b, hg = pl.program_id(0), pl.program_id(1)
    hp = SAMPLE_HEADS_PER_STEP
    ng = n_heads // hp
    step = b * ng + hg
    nstep = pl.num_programs(0) * ng

    def copies(bb, gg, slot, lookup):
        out = []
        for e in range(hp):
            hh = gg * hp + e
            for t in range(T):
                for s in range(n_sel):
                    blk = sel_ref[((bb * n_heads + hh) * T + t) * n_sel + s] if lookup else 0
                    for r in range(PAGES_PER_BLOCK):
                        page = pt_ref[bb, blk * PAGES_PER_BLOCK + r] if lookup else 0
                        dst = pl.ds((s * PAGES_PER_BLOCK + r) * PAGE_SIZE, PAGE_SIZE)
                        out.append(pltpu.make_async_copy(ck_hbm.at[page, hh], kbuf.at[slot, e, t, :, dst],
                                                         sems.at[slot]))
                        out.append(pltpu.make_async_copy(cv_hbm.at[page, hh], vbuf.at[slot, e, t, :, dst],
                                                         sems.at[slot]))
        return out

    @pl.when(step == 0)
    def _():
        for cp in copies(b, hg, 0, True):
            cp.start()

    @pl.when(step + 1 < nstep)
    def _():
        nxt = step + 1
        for cp in copies(nxt // ng, nxt % ng, nxt % 2, True):
            cp.start()

    slot = step % 2
    for cp in copies(0, 0, slot, False):
        cp.wait()

    lane_t = lax.broadcasted_iota(jnp.int32, (1, T), 1)
    lane = lax.broadcasted_iota(jnp.int32, (1, LANES), 1)
    for e in range(hp):
        qt = qt_ref[e]
        knt = knt_ref[e]
        vnt = vnt_ref[e]
        out = jnp.zeros((HEAD_DIM, LANES), F32)
        for t in range(T):
            qc = qt[:, t:t + 1]
            s_past = jnp.sum(kbuf[slot, e, t] * qc, axis=0, keepdims=True) * ATTN_SCALE
            s_new = jnp.sum(knt * qc, axis=0, keepdims=True) * ATTN_SCALE
            s_new = jnp.where(lane_t <= t, s_new, NEG)
            m = jnp.maximum(jnp.max(s_past, axis=-1, keepdims=True), jnp.max(s_new, axis=-1, keepdims=True))
            p_past = jnp.exp(s_past - m)
            p_new = jnp.exp(s_new - m)
            l = jnp.sum(p_past, axis=-1, keepdims=True) + jnp.sum(p_new, axis=-1, keepdims=True)
            o = (jnp.sum(vbuf[slot, e, t] * p_past, axis=-1, keepdims=True)
                 + jnp.sum(vnt * p_new, axis=-1, keepdims=True)) / l
            out = jnp.where(lane == t, o, out)
        o_ref[e] = out


def _sample_attn(sel_flat, page_table, qt, knt, vnt, ck, cv, n_sel):
    DB, H, Dh, T = qt.shape
    hp = SAMPLE_HEADS_PER_STEP
    assert H % hp == 0
    small = pl.BlockSpec((None, hp, Dh, T), lambda b, g, *_: (b, g, 0, 0))
    nkeys = n_sel * MOBA_BLOCK
    return pl.pallas_call(
        functools.partial(_sample_attn_kernel, T=T, n_sel=n_sel, n_heads=H),
        grid_spec=pltpu.PrefetchScalarGridSpec(
            num_scalar_prefetch=2,
            grid=(DB, H // hp),
            in_specs=[small, small, small, pl.BlockSpec(memory_space=pl.ANY), pl.BlockSpec(memory_space=pl.ANY)],
            out_specs=pl.BlockSpec((None, hp, Dh, LANES), lambda b, g, *_: (b, g, 0, 0)),
            scratch_shapes=[pltpu.VMEM((2, hp, T, Dh, nkeys), F32), pltpu.VMEM((2, hp, T, Dh, nkeys), F32),
                            pltpu.SemaphoreType.DMA((2,))],
        ),
        out_shape=jax.ShapeDtypeStruct((DB, H, Dh, LANES), F32),
        compiler_params=_cparams(("arbitrary", "arbitrary")),
        name="moba_sample",
    )(sel_flat, page_table, qt, knt, vnt, ck, cv)


def _moba_sample(q, kt, vt, ck, cv, page_table, km):
    DB, n_pages = page_table.shape
    T = q.shape[1] // DB
    nblk = n_pages // PAGES_PER_BLOCK
    n_sel = min(MOBA_TOPK, nblk)
    assert n_pages % PAGES_PER_BLOCK == 0 and n_sel == MOBA_TOPK
    qh = jnp.transpose(q.reshape(DB, T, N_HEADS, HEAD_DIM), (0, 2, 1, 3))
    qt = jnp.transpose(qh, (0, 1, 3, 2))
    knt = jnp.transpose(kt.reshape(N_HEADS, HEAD_DIM, DB, T), (2, 0, 1, 3))
    vnt = jnp.transpose(vt.reshape(N_HEADS, HEAD_DIM, DB, T), (2, 0, 1, 3))
    sel = _select(qh, km, nblk)[..., :n_sel].reshape(-1)
    o = _sample_attn(sel, page_table, qt, knt, vnt, ck, cv, n_sel)
    return jnp.transpose(o[..., :T], (0, 3, 1, 2)).reshape(DB * T, WIDTH)


def _to_jh(x):
    lead = x.shape[:-1]
    return jnp.swapaxes(x.reshape(*lead, N_HEADS, HEAD_DIM), -1, -2).reshape(*lead, WIDTH)


def _from_jh(x):
    lead = x.shape[:-1]
    return jnp.swapaxes(x.reshape(*lead, HEAD_DIM, N_HEADS), -1, -2).reshape(*lead, WIDTH)


def _shift_cols(x, reorder):
    parts = [reorder(x[..., i * WIDTH:(i + 1) * WIDTH]) for i in range(3)]
    return jnp.concatenate(parts + [x[..., 3 * WIDTH:]], axis=-1)


def _head_sum(x):
    z = x[:, 0:LANES]
    for p in range(1, x.shape[1] // LANES):
        z = z + x[:, p * LANES:(p + 1) * LANES]
    shift = N_HEADS
    while shift < LANES:
        z = z + pltpu.roll(z, shift, axis=1)
        shift *= 2
    return jnp.concatenate([z] * (x.shape[1] // LANES), axis=1)


def _rwkv_prep_kernel(pb_ref, prev_ref, mu_ref, w0_ref, w2p_ref, a0_ref, a2p_ref, g2_ref, kk_ref, ka_ref,
                      rk_ref, r_ref, w_ref, k_ref, v_ref, a_ref, b_ref, g_ref, bonus_ref, carry, *, transposed):
    c = pl.program_id(1)

    @pl.when(c == 0)
    def _():
        carry[...] = prev_ref[...]

    pb3 = pb_ref[...]
    G, Tc, P = pb3.shape
    tpos = lax.broadcasted_iota(jnp.int32, (1, Tc, 1), 1)
    prev3 = jnp.where(tpos == 0, carry[...], pltpu.roll(pb3, 1, axis=1))
    carry[...] = pb3[:, Tc - 1:Tc, :]
    pb = pb3.reshape(G * Tc, P)
    xm = pb + (prev3.reshape(G * Tc, P) - pb) * mu_ref[...]
    r = xm[:, 0:WIDTH]
    k = xm[:, WIDTH:2 * WIDTH]
    v = xm[:, 2 * WIDTH:3 * WIDTH]
    wa = xm[:, 3 * WIDTH:3 * WIDTH + D_DECAY_LORA + D_AAA_LORA]
    gd = xm[:, 3 * WIDTH + D_DECAY_LORA + D_AAA_LORA:]
    z = w0_ref[...] + jnp.dot(jnp.tanh(wa), w2p_ref[...], precision=HIGHEST, preferred_element_type=F32)
    softplus_negz = jnp.maximum(-z, 0.0) + jnp.log(1.0 + jnp.exp(-jnp.abs(z)))
    w = jnp.exp(-jnp.exp(-softplus_negz - 0.5))
    lr = _sigmoid(a0_ref[...] + jnp.dot(wa, a2p_ref[...], precision=HIGHEST, preferred_element_type=F32))
    g_ref[...] = jnp.dot(_sigmoid(gd), g2_ref[...], precision=HIGHEST, preferred_element_type=F32)
    kk = k * kk_ref[...]
    kk = kk / jnp.maximum(jnp.sqrt(_head_sum(kk * kk)), 1e-12)
    k2 = k * (1.0 + (lr - 1.0) * ka_ref[...])
    bonus_ref[...] = _head_sum(r * k2 * rk_ref[...]) * v
    outs = ((r_ref, r), (w_ref, w), (k_ref, k2), (v_ref, v), (a_ref, -kk), (b_ref, kk * lr))
    for ref, val in outs:
        ref[...] = val.T if transposed else val


def _rwkv_prep(pb, prev_row, consts, G, Tc, transposed):
    B, T, P = pb.shape
    nt = T // Tc
    assert G == 1 or nt == 1
    nat = jax.ShapeDtypeStruct((B * T, WIDTH), F32)
    nat_spec = pl.BlockSpec((G * Tc, WIDTH), lambda b, c: (b * nt + c, 0))
    if transposed:
        assert G == 1
        vec, vec_spec = jax.ShapeDtypeStruct((B, WIDTH, T), F32), pl.BlockSpec((None, WIDTH, Tc), lambda b, c: (b, 0, c))
    else:
        vec, vec_spec = nat, nat_spec
    return pl.pallas_call(
        functools.partial(_rwkv_prep_kernel, transposed=transposed),
        grid=(B // G, nt),
        in_specs=[pl.BlockSpec((G, Tc, P), lambda b, c: (b, c, 0)),
                  pl.BlockSpec((G, 1, P), lambda b, c: (b, 0, 0))] + [_const_spec(a.shape) for a in consts],
        out_specs=[vec_spec] * 6 + [nat_spec] * 2,
        out_shape=[vec] * 6 + [nat] * 2,
        scratch_shapes=[pltpu.VMEM((G, 1, P), F32)],
        compiler_params=_cparams(("parallel", "arbitrary")),
        name="rwkv_prep",
    )(pb, prev_row, *consts)


LANE_CHUNK = 128


def _to_lanes_kernel(x_ref, o_ref, *, tw):
    nb = x_ref.shape[0]
    for j in range(HEAD_DIM):
        m = x_ref[:, j * N_HEADS:(j + 1) * N_HEADS, :].reshape(nb * N_HEADS, LANE_CHUNK)
        mt = m.T
        for q in range(LANE_CHUNK // tw):
            o_ref[q, j] = mt[q * tw:(q + 1) * tw, :]


def _to_lanes(xt, tw):
    B, W, T = xt.shape
    L = B * N_HEADS
    assert L <= LANES and T % LANE_CHUNK == 0 and LANE_CHUNK % tw == 0
    return pl.pallas_call(
        functools.partial(_to_lanes_kernel, tw=tw),
        grid=(T // LANE_CHUNK,),
        in_specs=[pl.BlockSpec((B, W, LANE_CHUNK), lambda c: (0, 0, c))],
        out_specs=pl.BlockSpec((LANE_CHUNK // tw, HEAD_DIM, tw, L), lambda c: (c, 0, 0, 0)),
        out_shape=jax.ShapeDtypeStruct((T // tw, HEAD_DIM, tw, L), F32),
        compiler_params=_cparams(("parallel",)),
        name="to_lanes",
    )(xt)


def _from_lanes_kernel(y_ref, o_ref):
    nb = o_ref.shape[0]
    for i in range(HEAD_DIM):
        m = y_ref[pl.ds(i, LANE_CHUNK, stride=HEAD_DIM), :]
        o_ref[:, i * N_HEADS:(i + 1) * N_HEADS, :] = m.T.reshape(nb, N_HEADS, LANE_CHUNK)


def _from_lanes(y2d, B):
    TN, L = y2d.shape
    T = TN // HEAD_DIM
    assert L == B * N_HEADS <= LANES and T % LANE_CHUNK == 0
    return pl.pallas_call(
        _from_lanes_kernel,
        grid=(T // LANE_CHUNK,),
        in_specs=[pl.BlockSpec((LANE_CHUNK * HEAD_DIM, L), lambda c: (c, 0))],
        out_specs=pl.BlockSpec((B, WIDTH, LANE_CHUNK), lambda c: (0, 0, c)),
        out_shape=jax.ShapeDtypeStruct((B, WIDTH, T), F32),
        compiler_params=_cparams(("parallel",)),
        name="from_lanes",
    )(y2d)


WKV_UNROLL = 32


def _wkv_kernel(*refs, side_pages, side_steps, side_spq):
    if side_pages:
        refs = refs[1:]
    r_ref, w_ref, k_ref, v_ref, a_ref, b_ref, s0_ref = refs[:7]
    pages = refs[7:7 + side_pages]
    y_ref, sfin_ref = refs[7 + side_pages:9 + side_pages]
    S = refs[-1]
    c = pl.program_id(1)
    N, Tw = r_ref.shape[0], r_ref.shape[1]

    @pl.when(c == 0)
    def _():
        S[...] = s0_ref[...]

    if side_pages:
        @pl.when(c < side_steps)
        def _():
            _kmean_accumulate(pages, refs[9 + side_pages], c % side_spq)

    def token(t, carry):
        row = lambda ref, j: ref[j, pl.ds(t, 1), :]

        def sa_body(jb, sa):
            for jj in range(WKV_UNROLL):
                j = jb * WKV_UNROLL + jj
                sa = sa + S[j] * row(a_ref, j)
            return sa

        sa = lax.fori_loop(0, N // WKV_UNROLL, sa_body, jnp.zeros(S.shape[1:], F32))
        vt = v_ref[pl.ds(t, N, stride=Tw), :]

        def up_body(jb, y):
            for jj in range(WKV_UNROLL):
                j = jb * WKV_UNROLL + jj
                s = S[j] * row(w_ref, j) + sa * row(b_ref, j) + vt * row(k_ref, j)
                S[j] = s
                y = y + s * row(r_ref, j)
            return y

        y_ref[t] = lax.fori_loop(0, N // WKV_UNROLL, up_body, jnp.zeros(S.shape[1:], F32))
        return carry

    lax.fori_loop(0, Tw, token, 0)

    @pl.when(c == pl.num_programs(1) - 1)
    def _():
        sfin_ref[...] = S[...]


def _side_pages_per_step(page_table, n_steps):
    DB, n_pages = page_table.shape
    for pps in range(PAGES_PER_BLOCK, min(n_pages, 32) + 1, PAGES_PER_BLOCK):
        if n_pages % pps == 0 and DB * (n_pages // pps) <= n_steps:
            return pps
    return 0


def _wkv(r, w, k, v, a, b, s0, side=None):
    nc, N, Tw, L = r.shape
    lb = min(LANES, L)
    vec = pl.BlockSpec((None, N, Tw, lb), lambda g, c, *_: (c, 0, 0, g))
    st = pl.BlockSpec((N, N, lb), lambda g, c, *_: (0, 0, g))
    in_specs = [vec, vec, vec, pl.BlockSpec((N * Tw, lb), lambda g, c, *_: (c, g)), vec, vec, st]
    out_specs = [pl.BlockSpec((Tw, N, lb), lambda g, c, *_: (c, 0, g)), st]
    out_shape = [jax.ShapeDtypeStruct((nc * Tw, N, L), F32), jax.ShapeDtypeStruct((N, N, L), F32)]
    operands = [r, w, k, v.reshape(nc * N * Tw, L), a, b, s0]
    prefetch, pps, side_steps, spq = [], 0, 0, 1
    if side is not None:
        ck, page_table = side
        assert L == lb
        pps = _side_pages_per_step(page_table, nc)
        spq = page_table.shape[1] // pps
        side_steps = page_table.shape[0] * spq
        page_specs, km_spec = _kmean_specs(page_table.shape[1], pps, lambda g, c: jnp.minimum(c, side_steps - 1))
        in_specs += page_specs
        out_specs.append(km_spec)
        out_shape.append(jax.ShapeDtypeStruct((page_table.shape[0], N_HEADS, HEAD_DIM, LANES), F32))
        operands += [ck] * pps
        prefetch = [page_table]
    return pl.pallas_call(
        functools.partial(_wkv_kernel, side_pages=pps, side_steps=side_steps, side_spq=spq),
        grid_spec=pltpu.PrefetchScalarGridSpec(
            num_scalar_prefetch=len(prefetch), grid=(L // lb, nc), in_specs=in_specs, out_specs=out_specs,
            scratch_shapes=[pltpu.VMEM((N, N, lb), F32)]),
        out_shape=out_shape,
        compiler_params=_cparams(("parallel", "arbitrary")),
        name="rwkv_wkv",
    )(*prefetch, *operands)


def _rwkv(pb, prev_row, wkv0, consts, G, Tc, Tw, side=None):
    B, T, _ = pb.shape
    L = B * N_HEADS
    big = T % LANE_CHUNK == 0
    if side is not None and not (big and _side_pages_per_step(side[1], T // Tw)):
        side = None
    r, w, k, v, a, b, g, bonus = _rwkv_prep(pb, prev_row[:, None, :], consts, G, Tc, big)
    if big:
        coef = [_to_lanes(x, Tw) for x in (r, w, k, v, a, b)]
    else:
        assert Tw == T
        coef = jnp.stack([r, w, k, v, a, b]).reshape(6, B, T, HEAD_DIM, N_HEADS)
        coef = jnp.transpose(coef, (0, 3, 2, 1, 4)).reshape(6, 1, HEAD_DIM, T, L)
    s0 = jnp.transpose(wkv0, (3, 2, 0, 1)).reshape(HEAD_DIM, HEAD_DIM, L)
    y, sfin, *side_out = _wkv(*coef, s0, side)
    if big:
        y = _from_lanes(y.reshape(T * HEAD_DIM, L), B)
    else:
        y = jnp.transpose(y.reshape(T, HEAD_DIM, B, N_HEADS), (2, 0, 1, 3)).reshape(B * T, WIDTH)
    sfin = jnp.transpose(sfin.reshape(HEAD_DIM, HEAD_DIM, B, N_HEADS), (2, 3, 1, 0))
    return y, g, bonus, sfin, (side_out[0] if side_out else None)


def _merge_kernel(x_ref, at_ref, y_ref, bonus_ref, g_ref, sg_ref, lng_ref, lnb_ref,
                  wa_ref, wb_ref, wo_ref, h_ref, *, y_transposed):
    D = x_ref.shape[1]
    y = y_ref[...].T if y_transposed else y_ref[...]
    d = y - _head_sum(y) * (1.0 / HEAD_DIM)
    var = _head_sum(d * d) * (1.0 / HEAD_DIM)
    yn = d * lax.rsqrt(var + GN_EPS) * lng_ref[...] + lnb_ref[...]
    ob = (yn + bonus_ref[...]) * g_ref[...]
    y_a = jnp.dot(at_ref[...].astype(BF16), wa_ref[...], preferred_element_type=F32)
    y_b = jnp.dot(ob.astype(BF16), wb_ref[...], preferred_element_type=F32)
    mix = sg_ref[:, 0:D].astype(F32) * y_a + sg_ref[:, D:2 * D].astype(F32) * y_b
    h_ref[...] = x_ref[...] + jnp.dot(mix.astype(BF16), wo_ref[...], preferred_element_type=F32)


def _merge(x, attn, y, bonus, g, sg, lng, lnb, wa, wb, wo, tm, y_transposed):
    B, T, D = x.shape
    row = lambda w: pl.BlockSpec((None, tm, w), lambda b, t: (b, t, 0))
    yspec = pl.BlockSpec((None, WIDTH, tm), lambda b, t: (b, 0, t)) if y_transposed else row(WIDTH)
    consts = (lng, lnb, wa, wb, wo)
    return pl.pallas_call(
        functools.partial(_merge_kernel, y_transposed=y_transposed),
        grid=(B, T // tm),
        in_specs=[row(D), row(WIDTH), yspec, row(WIDTH), row(WIDTH), row(2 * D)]
                 + [_const_spec(a.shape) for a in consts],
        out_specs=row(D),
        out_shape=jax.ShapeDtypeStruct((B, T, D), F32),
        compiler_params=_cparams(("parallel", "parallel")),
        name="merge",
    )(x, attn, y, bonus, g, sg, *consts)


FFN_CHUNK = 256


def _ffn_kernel(h_ref, cp_ref, ln2_ref, wup_ref, cw_ref, cb_ref, wdn_ref, lnf_ref, y_ref, cs_ref, carry, act_ref,
                *, dff):
    c = pl.program_id(1)

    @pl.when(c == 0)
    def _():
        carry[...] = cp_ref[...]

    h3 = h_ref[...]
    G, Tc, D = h3.shape
    h = h3.reshape(G * Tc, D)
    xb = _rms(h, ln2_ref[...]).astype(BF16)
    thead = lax.broadcasted_iota(jnp.int32, (1, SUBLANES, 1), 1)

    def conv(cols):
        u = jnp.dot(xb, wup_ref[:, cols], preferred_element_type=F32).reshape(G, Tc, FFN_CHUNK)
        w0, w1, w2, cb = cw_ref[0:1, cols], cw_ref[1:2, cols], cw_ref[2:3, cols], cb_ref[:, cols]
        r1 = pltpu.roll(u, 1, axis=1)
        r2 = pltpu.roll(u, 2, axis=1)
        p0 = carry[:, 0:1, cols]
        p1 = carry[:, 1:2, cols]
        h1 = jnp.where(thead == 0, p1, r1[:, 0:SUBLANES, :])
        h2 = jnp.where(thead == 0, p0, jnp.where(thead == 1, p1, r2[:, 0:SUBLANES, :]))
        cv = cb + h2 * w0 + h1 * w1 + u[:, 0:SUBLANES, :] * w2
        if Tc > SUBLANES:
            rest = slice(SUBLANES, Tc)
            cv = jnp.concatenate([cv, cb + r2[:, rest, :] * w0 + r1[:, rest, :] * w1 + u[:, rest, :] * w2], axis=1)
        carry[:, :, cols] = u[:, Tc - (CONV_W - 1):Tc, :]
        return cv.reshape(G * Tc, FFN_CHUNK)

    for n in range(dff // FFN_CHUNK):
        val = conv(slice(n * FFN_CHUNK, (n + 1) * FFN_CHUNK))
        gt = conv(slice(dff + n * FFN_CHUNK, dff + (n + 1) * FFN_CHUNK))
        act_ref[:, n * FFN_CHUNK:(n + 1) * FFN_CHUNK] = (gt * _sigmoid(gt) * val).astype(BF16)
    cs_ref[...] = carry[...]
    out = h + jnp.dot(act_ref[...], wdn_ref[...], preferred_element_type=F32)
    y_ref[...] = _rms(out, lnf_ref[...]).reshape(G, Tc, D)


def _ffn(h, conv_prev, ln2, wup, cw, cb, wdn, lnf, G, Tc):
    B, T, D = h.shape
    dff = wdn.shape[0]
    assert dff % FFN_CHUNK == 0 and Tc >= CONV_W - 1 and (G == 1 or Tc == T)
    consts_a = (ln2, wup, cw, cb, wdn, lnf)
    tile = pl.BlockSpec((G, Tc, D), lambda b, c: (b, c, 0))
    cst = pl.BlockSpec((G, CONV_W - 1, 2 * dff), lambda b, c: (b, 0, 0))
    return pl.pallas_call(
        functools.partial(_ffn_kernel, dff=dff),
        grid=(B // G, T // Tc),
        in_specs=[tile, cst] + [_const_spec(a.shape) for a in consts_a],
        out_specs=[tile, cst],
        out_shape=[jax.ShapeDtypeStruct((B, T, D), F32), jax.ShapeDtypeStruct((B, CONV_W - 1, 2 * dff), F32)],
        scratch_shapes=[pltpu.VMEM((G, CONV_W - 1, 2 * dff), F32), pltpu.VMEM((G * Tc, dff), BF16)],
        compiler_params=_cparams(("parallel", "arbitrary")),
        name="convffn",
    )(h, conv_prev, *consts_a)


def _trunk(x, attend, shift_prev, wkv0, conv_prev, p, tiles):
    B, T, D = x.shape
    Bv, Tv = tiles["rows"]
    q, kt, vt, pb, sg = _inproj(x.reshape(Bv, Tv, D), p["ln1_g"], p["wq"], p["wkvt"], p["wpb"], p["wg"], tiles["tm"])
    attn = attend(q, kt, vt).reshape(Bv, Tv, WIDTH)
    pb = pb.reshape(B, T, RWKV_PROJ)
    y, g, bonus, sfin, side_out = _rwkv(pb, _shift_cols(shift_prev, _to_jh), wkv0, p["rwkv_consts"],
                                        tiles["G"], tiles["Tc"], tiles["Tw"], tiles.get("side"))
    y_transposed = y.ndim == 3
    if not y_transposed:
        y = y.reshape(Bv, Tv, WIDTH)
    h = _merge(x.reshape(Bv, Tv, D), attn, y, bonus.reshape(Bv, Tv, WIDTH), g.reshape(Bv, Tv, WIDTH), sg,
               p["lnx_g"], p["lnx_b"], p["w_br_a"], p["w_br_b"], p["w_o"], tiles["tm"], y_transposed)
    yout, conv_new = _ffn(h.reshape(B, T, D), conv_prev, p["ln2_g"], p["w_up"], p["conv_w"], p["conv_b"],
                          p["w_down"], p["lnf_g"], tiles["G"], tiles["Tc"])
    return yout, kt, vt, sfin, _shift_cols(pb[:, -1, :], _from_jh), conv_new, side_out


def kernel(x_prompt, x_sample, cache_k, cache_v, page_table, state_wkv, state_tshift, state_conv, ln1_g, w_in, w_br_a, mu, w0, w2, a0, a2, g2, k_k, k_a, r_k, lnx_g, lnx_b, w_br_b, w_o, ln2_g, w_up, conv_w, conv_b, w_down, lnf_g):
    assert w_in.shape[0] == 1, "single-layer trunk"
    BP, S, D = x_prompt.shape
    DB, TS, _ = x_sample.shape
    zrows = jnp.zeros((D_AAA_LORA, WIDTH), F32)
    wpb = _shift_cols(w_in[0, :, 3 * WIDTH:3 * WIDTH + RWKV_PROJ], _to_jh)
    p = dict(
        ln1_g=ln1_g, ln2_g=ln2_g, lnf_g=lnf_g[None, :], lnx_g=_to_jh(lnx_g), lnx_b=_to_jh(lnx_b),
        wq=w_in[0, :, 0:WIDTH].astype(BF16),
        wkvt=jnp.transpose(w_in[0, :, WIDTH:3 * WIDTH]).astype(BF16),
        wpb=wpb.astype(BF16),
        wg=w_in[0, :, 3 * WIDTH + RWKV_PROJ:].astype(BF16),
        w_br_a=w_br_a[0].astype(BF16), w_br_b=_to_jh(w_br_b[0].T).T.astype(BF16), w_o=w_o[0].astype(BF16),
        w_up=w_up[0].astype(BF16), w_down=w_down[0].astype(BF16), conv_w=conv_w[0], conv_b=conv_b,
    )
    p["rwkv_consts"] = (_shift_cols(mu, _to_jh), _to_jh(w0), _to_jh(jnp.concatenate([w2[0], zrows], axis=0)),
                        _to_jh(a0), _to_jh(jnp.concatenate([zrows, a2[0]], axis=0)), _to_jh(g2[0]),
                        _to_jh(k_k), _to_jh(k_a), _to_jh(r_k.reshape(1, WIDTH)))

    ck = jnp.transpose(cache_k[0], (0, 2, 3, 1))
    cv = jnp.transpose(cache_v[0], (0, 2, 3, 1))

    tiles_p = dict(rows=(BP, S), tm=min(512, S), G=1, Tc=min(512, S), Tw=min(16, S), side=(ck, page_table))
    yp, ktp, vtp, wkv_p, tshift_p, conv_p, km = _trunk(
        x_prompt, lambda q, kt, vt: _moba_prompt(q, kt, vt),
        jnp.zeros((BP, RWKV_PROJ), F32), jnp.zeros((BP, N_HEADS, HEAD_DIM, HEAD_DIM), F32),
        jnp.zeros((BP, CONV_W - 1, w_up.shape[2]), F32), p, tiles_p)
    if km is None:
        km = _kmean(ck, page_table)

    tiles_s = dict(rows=(1, DB * TS), tm=DB * TS, G=DB, Tc=TS, Tw=TS)
    attend_s = lambda q, kt, vt: _moba_sample(q, kt, vt, ck, cv, page_table, km)
    ys, kts, vts, wkv_s, tshift_s, conv_s, _ = _trunk(
        x_sample, attend_s, state_tshift[0], state_wkv[0], state_conv[0], p, tiles_s)

    heads_p = lambda t: jnp.transpose(t.reshape(BP, N_HEADS, HEAD_DIM, S), (0, 3, 1, 2))[None]
    heads_s = lambda t: jnp.transpose(t.reshape(N_HEADS, HEAD_DIM, DB, TS), (2, 3, 0, 1))[None]
    return (yp, ys, heads_p(ktp), heads_p(vtp), wkv_p[None], tshift_p[None], conv_p[None],
            heads_s(kts), heads_s(vts), wkv_s[None], tshift_s[None], conv_s[None])
```

```python
import functools

import jax
import jax.numpy as jnp
from jax import lax
from jax.experimental import pallas as pl
from jax.experimental.pallas import tpu as pltpu

F32 = jnp.float32
BF16 = jnp.bfloat16
HIGHEST = lax.Precision.HIGHEST
NT_DIMS = (((1,), (1,)), ((), ()))

HEAD_DIM = 64
N_HEADS = 8
WIDTH = N_HEADS * HEAD_DIM
MOBA_BLOCK = 256
MOBA_TOPK = 3
PAGE_SIZE = 128
PAGES_PER_BLOCK = MOBA_BLOCK // PAGE_SIZE
D_DECAY_LORA = 64
D_AAA_LORA = 64
D_GATE_LORA = 128
RWKV_PROJ = 3 * WIDTH + D_DECAY_LORA + D_AAA_LORA + D_GATE_LORA
CONV_W = 3
RMS_EPS = 1e-6
GN_EPS = 64e-5
NEG = -1e30
ATTN_SCALE = HEAD_DIM ** -0.5

LANES = 128
SUBLANES = 8
VMEM_LIMIT = 56 * 1024 * 1024


def _cparams(sem):
    return pltpu.CompilerParams(dimension_semantics=sem, vmem_limit_bytes=VMEM_LIMIT)


def _const_spec(shape):
    nd = len(shape)
    return pl.BlockSpec(shape, lambda *_: (0,) * nd, pipeline_mode=pl.Buffered(1))


def _sigmoid(x):
    return 1.0 / (1.0 + jnp.exp(-x))


def _rms(x, g):
    return x * lax.rsqrt(jnp.mean(x * x, axis=-1, keepdims=True) + RMS_EPS) * g


def _inproj_kernel(x_ref, g_ref, wq_ref, wkvt_ref, wpb_ref, wg_ref,
                   q_ref, kt_ref, vt_ref, pb_ref, sg_ref):
    xb = _rms(x_ref[...], g_ref[...]).astype(BF16)
    q_ref[...] = jnp.dot(xb, wq_ref[...], preferred_element_type=F32)
    kt_ref[...] = lax.dot_general(wkvt_ref[0:WIDTH, :], xb, NT_DIMS, preferred_element_type=F32)
    vt_ref[...] = lax.dot_general(wkvt_ref[WIDTH:2 * WIDTH, :], xb, NT_DIMS, preferred_element_type=F32)
    pb_ref[...] = jnp.dot(xb, wpb_ref[...], preferred_element_type=F32)
    sg_ref[...] = _sigmoid(jnp.dot(xb, wg_ref[...], preferred_element_type=F32)).astype(sg_ref.dtype)


def _inproj(x, ln_g, wq, wkvt, wpb, wg, tm):
    B, T, D = x.shape
    row = lambda w: pl.BlockSpec((None, tm, w), lambda b, t: (b, t, 0))
    col = pl.BlockSpec((None, WIDTH, tm), lambda b, t: (b, 0, t))
    return pl.pallas_call(
        _inproj_kernel,
        grid=(B, T // tm),
        in_specs=[row(D), _const_spec((1, D)), _const_spec(wq.shape), _const_spec(wkvt.shape),
                  _const_spec(wpb.shape), _const_spec(wg.shape)],
        out_specs=[row(WIDTH), col, col, row(RWKV_PROJ), row(2 * D)],
        out_shape=[jax.ShapeDtypeStruct((B, T, WIDTH), F32),
                   jax.ShapeDtypeStruct((B, WIDTH, T), F32),
                   jax.ShapeDtypeStruct((B, WIDTH, T), F32),
                   jax.ShapeDtypeStruct((B, T, RWKV_PROJ), F32),
                   jax.ShapeDtypeStruct((B, T, 2 * D), BF16)],
        compiler_params=_cparams(("parallel", "parallel")),
        name="inproj",
    )(x, ln_g, wq, wkvt, wpb, wg)


def _moba_prompt_kernel(q_ref, kt_ref, vt_ref, o_ref, kp_ref, vtb_ref, qa_ref, s_ref, p_ref, *, nblk):
    S = q_ref.shape[0]
    kt = kt_ref[...]
    vtb_ref[...] = vt_ref[...].astype(BF16)
    lane = lax.broadcasted_iota(jnp.int32, (1, LANES), 1)
    rowd = lax.broadcasted_iota(jnp.int32, (LANES, 1), 0)
    keyblk = lax.broadcasted_iota(jnp.int32, (1, S), 1) // MOBA_BLOCK
    km = jnp.zeros((LANES, LANES), F32)
    first = rowd < HEAD_DIM
    for j in range(nblk):
        colj = jnp.sum(kt[:, j * MOBA_BLOCK:(j + 1) * MOBA_BLOCK], axis=1, keepdims=True) * (1.0 / MOBA_BLOCK)
        km = jnp.where(((lane == j) & jnp.logical_not(first)) | ((lane == HEAD_DIM + j) & first), colj, km)
    qrow = lax.broadcasted_iota(jnp.int32, (MOBA_BLOCK, 1), 0)
    kcol = lax.broadcasted_iota(jnp.int32, (1, MOBA_BLOCK), 1)
    for hh in range(2):
        in_r = (rowd >= HEAD_DIM * hh) & (rowd < HEAD_DIM * (hh + 1))
        ind = jnp.where(rowd - HEAD_DIM * (1 - hh) == keyblk, 1.0, 0.0)
        kp_ref[hh] = jnp.where(in_r, kt, ind).astype(BF16)
    for i in range(nblk):
        rows = slice(i * MOBA_BLOCK, (i + 1) * MOBA_BLOCK)
        qi = q_ref[rows, :]
        if i > 0:
            gate = jnp.dot(qi, km, precision=HIGHEST, preferred_element_type=F32)
        for hh in range(2):
            in_l = (lane >= HEAD_DIM * hh) & (lane < HEAD_DIM * (hh + 1))
            base = HEAD_DIM * (1 - hh)
            jl = lane - base
            is_bias_lane = (jl >= 0) & (jl < nblk)
            if i > 0:
                rank = jnp.zeros((MOBA_BLOCK, LANES), F32)
                for jp in range(i):
                    cj = gate[:, base + jp:base + jp + 1]
                    beats = (cj > gate) | ((cj == gate) & (jp < jl))
                    rank = rank + jnp.where(beats, 1.0, 0.0)
                keep = ((jl < i) & (rank < min(MOBA_TOPK, nblk - 1))) | (jl == i)
            else:
                keep = jl == 0
            bias = jnp.where(is_bias_lane & jnp.logical_not(keep), NEG, 0.0)
            qa_ref[hh, rows, :] = jnp.where(in_l, qi * ATTN_SCALE, bias).astype(BF16)
    for i in range(nblk):
        rows = slice(i * MOBA_BLOCK, (i + 1) * MOBA_BLOCK)
        outs = []
        for hh in range(2):
            qa = qa_ref[hh, rows, :]
            sb, pb = s_ref.at[hh], p_ref.at[hh]
            nk = (i + 1) * MOBA_BLOCK
            mx = None
            for j in range(i + 1):
                keys = slice(j * MOBA_BLOCK, (j + 1) * MOBA_BLOCK)
                s = jnp.dot(qa, kp_ref[hh, :, keys], preferred_element_type=F32)
                if j == i:
                    s = jnp.where(kcol <= qrow, s, NEG)
                sb[:, keys] = s
                e = jnp.maximum(s[:, 0:LANES], s[:, LANES:2 * LANES])
                mx = e if mx is None else jnp.maximum(mx, e)
            m = jnp.max(mx, axis=-1, keepdims=True)
            ls = None
            for j in range(i + 1):
                keys = slice(j * MOBA_BLOCK, (j + 1) * MOBA_BLOCK)
                p = jnp.exp(sb[:, keys] - m)
                pb[:, keys] = p.astype(BF16)
                e = p[:, 0:LANES] + p[:, LANES:2 * LANES]
                ls = e if ls is None else ls + e
            l = jnp.sum(ls, axis=-1, keepdims=True)
            outs.append(lax.dot_general(pb[:, 0:nk], vtb_ref[:, 0:nk], NT_DIMS, preferred_element_type=F32) / l)
        o_ref[rows, :] = jnp.where(lane < HEAD_DIM, outs[0], outs[1]).astype(o_ref.dtype)


def _moba_prompt(q, kt, vt):
    B, S, _ = q.shape
    nblk = S // MOBA_BLOCK
    qspec = pl.BlockSpec((None, S, LANES), lambda b, h: (b, 0, h))
    tspec = pl.BlockSpec((None, LANES, S), lambda b, h: (b, h, 0))
    return pl.pallas_call(
        functools.partial(_moba_prompt_kernel, nblk=nblk),
        grid=(B, WIDTH // LANES),
        in_specs=[qspec, tspec, tspec],
        out_specs=qspec,
        out_shape=jax.ShapeDtypeStruct((B, S, WIDTH), BF16),
        scratch_shapes=[pltpu.VMEM((2, LANES, S), BF16), pltpu.VMEM((LANES, S), BF16), pltpu.VMEM((2, S, LANES), BF16),
                        pltpu.VMEM((2, MOBA_BLOCK, S), F32), pltpu.VMEM((2, MOBA_BLOCK, S), BF16)],
        compiler_params=_cparams(("parallel", "parallel")),
        name="moba_prompt",
    )(q, kt, vt)


def _kmean_accumulate(pages, km_ref, chunk):
    @pl.when(chunk == 0)
    def _():
        km_ref[...] = jnp.zeros(km_ref.shape, F32)

    lane = lax.broadcasted_iota(jnp.int32, (1, 1, LANES), 2)
    blocks_per_step = len(pages) // PAGES_PER_BLOCK
    for jj in range(blocks_per_step):
        tot = pages[PAGES_PER_BLOCK * jj][...]
        for r in range(1, PAGES_PER_BLOCK):
            tot = tot + pages[PAGES_PER_BLOCK * jj + r][...]
        colj = jnp.sum(tot, axis=-1, keepdims=True) * (1.0 / MOBA_BLOCK)
        km_ref[...] = jnp.where(lane == chunk * blocks_per_step + jj, colj, km_ref[...])


def _kmean_specs(n_pages, pages_per_step, step_of):
    spq = n_pages // pages_per_step

    def page_spec(r):
        def index(*ids):
            step, pt = step_of(*ids[:-1]), ids[-1]
            return (pt[step // spq, (step % spq) * pages_per_step + r], 0, 0, 0)
        return pl.BlockSpec((None, N_HEADS, HEAD_DIM, PAGE_SIZE), index)

    km_spec = pl.BlockSpec((None, N_HEADS, HEAD_DIM, LANES), lambda *ids: (step_of(*ids[:-1]) // spq, 0, 0, 0))
    return [page_spec(r) for r in range(pages_per_step)], km_spec


def _kmean_kernel(pt_ref, *refs):
    del pt_ref
    _kmean_accumulate(refs[:-1], refs[-1], pl.program_id(1))


def _kmean(ck, page_table, pages_per_step=32):
    DB, n_pages = page_table.shape
    pages_per_step = min(pages_per_step, n_pages)
    assert n_pages // PAGES_PER_BLOCK <= LANES and n_pages % pages_per_step == 0
    spq = n_pages // pages_per_step
    page_specs, km_spec = _kmean_specs(n_pages, pages_per_step, lambda b, c: b * spq + c)
    return pl.pallas_call(
        _kmean_kernel,
        grid_spec=pltpu.PrefetchScalarGridSpec(
            num_scalar_prefetch=1, grid=(DB, spq), in_specs=page_specs, out_specs=km_spec),
        out_shape=jax.ShapeDtypeStruct((DB, N_HEADS, HEAD_DIM, LANES), F32),
        compiler_params=_cparams(("parallel", "arbitrary")),
        name="moba_kmean",
    )(page_table, *([ck] * pages_per_step))


def _select_kernel(q_ref, km_ref, sel_ref, *, nblk):
    lane = lax.broadcasted_iota(jnp.int32, (1, LANES), 1)
    lanef = lane.astype(F32)
    for h in range(N_HEADS):
        gate = jnp.dot(q_ref[h], km_ref[h], precision=HIGHEST, preferred_element_type=F32)
        gate = jnp.where(lane < nblk, gate, -jnp.inf)
        out = jnp.zeros(gate.shape, F32)
        for r in range(MOBA_TOPK):
            m = jnp.max(gate, axis=-1, keepdims=True)
            idx = jnp.min(jnp.where(gate == m, lanef, float(LANES)), axis=-1, keepdims=True)
            out = jnp.where(lane == r, idx, out)
            gate = jnp.where(lanef == idx, -jnp.inf, gate)
        sel_ref[h] = out.astype(jnp.int32)


def _select(qh, km, nblk):
    DB, H, T, Dh = qh.shape
    return pl.pallas_call(
        functools.partial(_select_kernel, nblk=nblk),
        grid=(DB,),
        in_specs=[pl.BlockSpec((None, H, T, Dh), lambda b: (b, 0, 0, 0)),
                  pl.BlockSpec((None, H, Dh, LANES), lambda b: (b, 0, 0, 0))],
        out_specs=pl.BlockSpec((None, H, T, LANES), lambda b: (b, 0, 0, 0)),
        out_shape=jax.ShapeDtypeStruct((DB, H, T, LANES), jnp.int32),
        compiler_params=_cparams(("parallel",)),
        name="moba_select",
    )(qh, km)


SAMPLE_HEADS_PER_STEP = 1


def _sample_attn_kernel(sel_ref, pt_ref, qt_ref, knt_ref, vnt_ref, ck_hbm, cv_hbm, o_ref, kbuf, vbuf, sems,
                        *, T, n_sel, n_heads):
    b, hg = pl.program_id(0), pl.program_id(1)
    hp = SAMPLE_HEADS_PER_STEP
    ng = n_heads // hp
    step = b * ng + hg
    nstep = pl.num_programs(0) * ng

    def copies(bb, gg, slot, lookup):
        out = []
        for e in range(hp):
            hh = gg * hp + e
            for t in range(T):
                for s in range(n_sel):
                    blk = sel_ref[((bb * n_heads + hh) * T + t) * n_sel + s] if lookup else 0
                    for r in range(PAGES_PER_BLOCK):
                        page = pt_ref[bb, blk * PAGES_PER_BLOCK + r] if lookup else 0
                        dst = pl.ds((s * PAGES_PER_BLOCK + r) * PAGE_SIZE, PAGE_SIZE)
                        out.append(pltpu.make_async_copy(ck_hbm.at[page, hh], kbuf.at[slot, e, t, :, dst],
                                                         sems.at[slot]))
                        out.append(pltpu.make_async_copy(cv_hbm.at[page, hh], vbuf.at[slot, e, t, :, dst],
                                                         sems.at[slot]))
        return out

    @pl.when(step == 0)
    def _():
        for cp in copies(b, hg, 0, True):
            cp.start()

    @pl.when(step + 1 < nstep)
    def _():
        nxt = step + 1
        for cp in copies(nxt // ng, nxt % ng, nxt % 2, True):
            cp.start()

    slot = step % 2
    for cp in copies(0, 0, slot, False):
        cp.wait()

    lane_t = lax.broadcasted_iota(jnp.int32, (1, T), 1)
    row_t = lax.broadcasted_iota(jnp.int32, (T, 1), 0)
    lane = lax.broadcasted_iota(jnp.int32, (1, LANES), 1)
    for e in range(hp):
        qt = qt_ref[e]
        knt = knt_ref[e]
        vnt = vnt_ref[e]
        s_past, s_new = [], []
        for t in range(T):
            qc = qt[:, t:t + 1]
            s_past.append(jnp.sum(kbuf[slot, e, t] * qc, axis=0, keepdims=True))
            s_new.append(jnp.sum(knt * qc, axis=0, keepdims=True))
        s_past = jnp.concatenate(s_past, axis=0) * ATTN_SCALE
        s_new = jnp.concatenate(s_new, axis=0) * ATTN_SCALE
        s_new = jnp.where(lane_t <= row_t, s_new, NEG)
        m = jnp.maximum(jnp.max(s_past, axis=-1, keepdims=True), jnp.max(s_new, axis=-1, keepdims=True))
        p_past = jnp.exp(s_past - m)
        p_new = jnp.exp(s_new - m)
        l = jnp.sum(p_past, axis=-1, keepdims=True) + jnp.sum(p_new, axis=-1, keepdims=True)
        out = jnp.zeros((HEAD_DIM, LANES), F32)
        for t in range(T):
            o = (jnp.sum(vbuf[slot, e, t] * p_past[t:t + 1, :], axis=-1, keepdims=True)
                 + jnp.sum(vnt * p_new[t:t + 1, :], axis=-1, keepdims=True)) / l[t:t + 1, :]
            out = jnp.where(lane == t, o, out)
        o_ref[e] = out


def _sample_attn(sel_flat, page_table, qt, knt, vnt, ck, cv, n_sel):
    DB, H, Dh, T = qt.shape
    hp = SAMPLE_HEADS_PER_STEP
    assert H % hp == 0
    small = pl.BlockSpec((None, hp, Dh, T), lambda b, g, *_: (b, g, 0, 0))
    nkeys = n_sel * MOBA_BLOCK
    return pl.pallas_call(
        functools.partial(_sample_attn_kernel, T=T, n_sel=n_sel, n_heads=H),
        grid_spec=pltpu.PrefetchScalarGridSpec(
            num_scalar_prefetch=2,
            grid=(DB, H // hp),
            in_specs=[small, small, small, pl.BlockSpec(memory_space=pl.ANY), pl.BlockSpec(memory_space=pl.ANY)],
            out_specs=pl.BlockSpec((None, hp, Dh, LANES), lambda b, g, *_: (b, g, 0, 0)),
            scratch_shapes=[pltpu.VMEM((2, hp, T, Dh, nkeys), F32), pltpu.VMEM((2, hp, T, Dh, nkeys), F32),
                            pltpu.SemaphoreType.DMA((2,))],
        ),
        out_shape=jax.ShapeDtypeStruct((DB, H, Dh, LANES), F32),
        compiler_params=_cparams(("arbitrary", "arbitrary")),
        name="moba_sample",
    )(sel_flat, page_table, qt, knt, vnt, ck, cv)


def _moba_sample(q, kt, vt, ck, cv, page_table, km):
    DB, n_pages = page_table.shape
    T = q.shape[1] // DB
    nblk = n_pages // PAGES_PER_BLOCK
    n_sel = min(MOBA_TOPK, nblk)
    assert n_pages % PAGES_PER_BLOCK == 0 and n_sel == MOBA_TOPK
    qh = jnp.transpose(q.reshape(DB, T, N_HEADS, HEAD_DIM), (0, 2, 1, 3))
    qt = jnp.transpose(qh, (0, 1, 3, 2))
    knt = jnp.transpose(kt.reshape(N_HEADS, HEAD_DIM, DB, T), (2, 0, 1, 3))
    vnt = jnp.transpose(vt.reshape(N_HEADS, HEAD_DIM, DB, T), (2, 0, 1, 3))
    sel = _select(qh, km, nblk)[..., :n_sel].reshape(-1)
    o = _sample_attn(sel, page_table, qt, knt, vnt, ck, cv, n_sel)
    return jnp.transpose(o[..., :T], (0, 3, 1, 2)).reshape(DB * T, WIDTH)


def _to_jh(x):
    lead = x.shape[:-1]
    return jnp.swapaxes(x.reshape(*lead, N_HEADS, HEAD_DIM), -1, -2).reshape(*lead, WIDTH)


def _from_jh(x):
    lead = x.shape[:-1]
    return jnp.swapaxes(x.reshape(*lead, HEAD_DIM, N_HEADS), -1, -2).reshape(*lead, WIDTH)


def _shift_cols(x, reorder):
    parts = [reorder(x[..., i * WIDTH:(i + 1) * WIDTH]) for i in range(3)]
    return jnp.concatenate(parts + [x[..., 3 * WIDTH:]], axis=-1)


def _head_sum(x):
    z = x[:, 0:LANES]
    for p in range(1, x.shape[1] // LANES):
        z = z + x[:, p * LANES:(p + 1) * LANES]
    shift = N_HEADS
    while shift < LANES:
        z = z + pltpu.roll(z, shift, axis=1)
        shift *= 2
    return jnp.concatenate([z] * (x.shape[1] // LANES), axis=1)


def _rwkv_prep_kernel(pb_ref, prev_ref, mu_ref, w0_ref, w2p_ref, a0_ref, a2p_ref, g2_ref, kk_ref, ka_ref,
                      rk_ref, r_ref, w_ref, k_ref, v_ref, a_ref, b_ref, g_ref, bonus_ref, carry, *, transposed):
    c = pl.program_id(1)

    @pl.when(c == 0)
    def _():
        carry[...] = prev_ref[...]

    pb3 = pb_ref[...]
    G, Tc, P = pb3.shape
    tpos = lax.broadcasted_iota(jnp.int32, (1, Tc, 1), 1)
    prev3 = jnp.where(tpos == 0, carry[...], pltpu.roll(pb3, 1, axis=1))
    carry[...] = pb3[:, Tc - 1:Tc, :]
    pb = pb3.reshape(G * Tc, P)
    xm = pb + (prev3.reshape(G * Tc, P) - pb) * mu_ref[...]
    r = xm[:, 0:WIDTH]
    k = xm[:, WIDTH:2 * WIDTH]
    v = xm[:, 2 * WIDTH:3 * WIDTH]
    wa = xm[:, 3 * WIDTH:3 * WIDTH + D_DECAY_LORA + D_AAA_LORA]
    gd = xm[:, 3 * WIDTH + D_DECAY_LORA + D_AAA_LORA:]
    z = w0_ref[...] + jnp.dot(jnp.tanh(wa), w2p_ref[...], precision=HIGHEST, preferred_element_type=F32)
    softplus_negz = jnp.maximum(-z, 0.0) + jnp.log(1.0 + jnp.exp(-jnp.abs(z)))
    w = jnp.exp(-jnp.exp(-softplus_negz - 0.5))
    lr = _sigmoid(a0_ref[...] + jnp.dot(wa, a2p_ref[...], precision=HIGHEST, preferred_element_type=F32))
    g_ref[...] = jnp.dot(_sigmoid(gd), g2_ref[...], precision=HIGHEST, preferred_element_type=F32)
    kk = k * kk_ref[...]
    kk = kk / jnp.maximum(jnp.sqrt(_head_sum(kk * kk)), 1e-12)
    k2 = k * (1.0 + (lr - 1.0) * ka_ref[...])
    bonus_ref[...] = _head_sum(r * k2 * rk_ref[...]) * v
    outs = ((r_ref, r), (w_ref, w), (k_ref, k2), (v_ref, v), (a_ref, -kk), (b_ref, kk * lr))
    for ref, val in outs:
        ref[...] = val.T if transposed else val


def _rwkv_prep(pb, prev_row, consts, G, Tc, transposed):
    B, T, P = pb.shape
    nt = T // Tc
    assert G == 1 or nt == 1
    nat = jax.ShapeDtypeStruct((B * T, WIDTH), F32)
    nat_spec = pl.BlockSpec((G * Tc, WIDTH), lambda b, c: (b * nt + c, 0))
    if transposed:
        assert G == 1
        vec, vec_spec = jax.ShapeDtypeStruct((B, WIDTH, T), F32), pl.BlockSpec((None, WIDTH, Tc), lambda b, c: (b, 0, c))
    else:
        vec, vec_spec = nat, nat_spec
    return pl.pallas_call(
        functools.partial(_rwkv_prep_kernel, transposed=transposed),
        grid=(B // G, nt),
        in_specs=[pl.BlockSpec((G, Tc, P), lambda b, c: (b, c, 0)),
                  pl.BlockSpec((G, 1, P), lambda b, c: (b, 0, 0))] + [_const_spec(a.shape) for a in consts],
        out_specs=[vec_spec] * 6 + [nat_spec] * 2,
        out_shape=[vec] * 6 + [nat] * 2,
        scratch_shapes=[pltpu.VMEM((G, 1, P), F32)],
        compiler_params=_cparams(("parallel", "arbitrary")),
        name="rwkv_prep",
    )(pb, prev_row, *consts)


LANE_CHUNK = 128


def _to_lanes_kernel(x_ref, o_ref, *, tw):
    nb = x_ref.shape[0]
    for j in range(HEAD_DIM):
        m = x_ref[:, j * N_HEADS:(j + 1) * N_HEADS, :].reshape(nb * N_HEADS, LANE_CHUNK)
        mt = m.T
        for q in range(LANE_CHUNK // tw):
            o_ref[q, j] = mt[q * tw:(q + 1) * tw, :]


def _to_lanes(xt, tw):
    B, W, T = xt.shape
    L = B * N_HEADS
    assert L <= LANES and T % LANE_CHUNK == 0 and LANE_CHUNK % tw == 0
    return pl.pallas_call(
        functools.partial(_to_lanes_kernel, tw=tw),
        grid=(T // LANE_CHUNK,),
        in_specs=[pl.BlockSpec((B, W, LANE_CHUNK), lambda c: (0, 0, c))],
        out_specs=pl.BlockSpec((LANE_CHUNK // tw, HEAD_DIM, tw, L), lambda c: (c, 0, 0, 0)),
        out_shape=jax.ShapeDtypeStruct((T // tw, HEAD_DIM, tw, L), F32),
        compiler_params=_cparams(("parallel",)),
        name="to_lanes",
    )(xt)


def _from_lanes_kernel(y_ref, o_ref):
    nb = o_ref.shape[0]
    for i in range(HEAD_DIM):
        m = y_ref[pl.ds(i, LANE_CHUNK, stride=HEAD_DIM), :]
        o_ref[:, i * N_HEADS:(i + 1) * N_HEADS, :] = m.T.reshape(nb, N_HEADS, LANE_CHUNK)


def _from_lanes(y2d, B):
    TN, L = y2d.shape
    T = TN // HEAD_DIM
    assert L == B * N_HEADS <= LANES and T % LANE_CHUNK == 0
    return pl.pallas_call(
        _from_lanes_kernel,
        grid=(T // LANE_CHUNK,),
        in_specs=[pl.BlockSpec((LANE_CHUNK * HEAD_DIM, L), lambda c: (c, 0))],
        out_specs=pl.BlockSpec((B, WIDTH, LANE_CHUNK), lambda c: (0, 0, c)),
        out_shape=jax.ShapeDtypeStruct((B, WIDTH, T), F32),
        compiler_params=_cparams(("parallel",)),
        name="from_lanes",
    )(y2d)


WKV_UNROLL = 32


def _wkv_kernel(*refs, side_pages, side_steps, side_spq):
    if side_pages:
        pt_ref, refs = refs[0], refs[1:]
        r_ref, w_ref, k_ref, v_ref, a_ref, b_ref, s0_ref, ck_hbm, y_ref, sfin_ref, km_ref, S, pbuf, sems = refs
    else:
        r_ref, w_ref, k_ref, v_ref, a_ref, b_ref, s0_ref, y_ref, sfin_ref, S = refs
    c = pl.program_id(1)
    N, Tw = r_ref.shape[0], r_ref.shape[1]
    trips = N // WKV_UNROLL

    @pl.when(c == 0)
    def _():
        S[...] = s0_ref[...]

    if side_pages:
        def page_copies(s, lookup):
            seq, first = s // side_spq, (s % side_spq) * side_pages
            return [pltpu.make_async_copy(ck_hbm.at[pt_ref[seq, first + r] if lookup else 0],
                                          pbuf.at[s % 2, r], sems.at[s % 2]) for r in range(side_pages)]

        @pl.when(c == 0)
        def _():
            for cp in page_copies(c, True):
                cp.start()

        @pl.when(c + 1 < side_steps)
        def _():
            for cp in page_copies(c + 1, True):
                cp.start()

        @pl.when(c < side_steps)
        def _():
            for cp in page_copies(c, False):
                cp.wait()
            _kmean_accumulate([pbuf.at[c % 2, r] for r in range(side_pages)], km_ref, c % side_spq)

    def token(t, carry):
        row = lambda ref, j: ref[j, pl.ds(t, 1), :]

        def sa_body(jb, sa):
            for jj in range(WKV_UNROLL):
                j = jb * WKV_UNROLL + jj
                sa = sa + S[j] * row(a_ref, j)
            return sa

        sa = lax.fori_loop(0, trips, sa_body, jnp.zeros(S.shape[1:], F32))
        vt = v_ref[pl.ds(t, N, stride=Tw), :]

        def up_body(jb, y):
            for jj in range(WKV_UNROLL):
                j = jb * WKV_UNROLL + jj
                s = S[j] * row(w_ref, j) + sa * row(b_ref, j) + vt * row(k_ref, j)
                S[j] = s
                y = y + s * row(r_ref, j)
            return y

        y_ref[t] = lax.fori_loop(0, N // WKV_UNROLL, up_body, jnp.zeros(S.shape[1:], F32))
        return carry

    lax.fori_loop(0, Tw, token, 0)

    @pl.when(c == pl.num_programs(1) - 1)
    def _():
        sfin_ref[...] = S[...]


def _side_pages_per_step(page_table, n_steps, tokens_per_step):
    DB, n_pages = page_table.shape
    for pps in range(PAGES_PER_BLOCK, min(n_pages, tokens_per_step * PAGES_PER_BLOCK) + 1, PAGES_PER_BLOCK):
        if n_pages % pps == 0 and 2 <= DB * (n_pages // pps) <= n_steps:
            return pps
    return 0


def _wkv(r, w, k, v, a, b, s0, side=None):
    nc, N, Tw, L = r.shape
    lb = min(LANES, L)
    vec = pl.BlockSpec((None, N, Tw, lb), lambda g, c, *_: (c, 0, 0, g))
    st = pl.BlockSpec((N, N, lb), lambda g, c, *_: (0, 0, g))
    in_specs = [vec, vec, vec, pl.BlockSpec((N * Tw, lb), lambda g, c, *_: (c, g)), vec, vec, st]
    out_specs = [pl.BlockSpec((Tw, N, lb), lambda g, c, *_: (c, 0, g)), st]
    out_shape = [jax.ShapeDtypeStruct((nc * Tw, N, L), F32), jax.ShapeDtypeStruct((N, N, L), F32)]
    operands = [r, w, k, v.reshape(nc * N * Tw, L), a, b, s0]
    scratch = [pltpu.VMEM((N, N, lb), F32)]
    prefetch, pps, side_steps, spq = [], 0, 0, 1
    if side is not None:
        ck, page_table = side
        assert L == lb
        pps = _side_pages_per_step(page_table, nc, Tw)
        spq = page_table.shape[1] // pps
        side_steps = page_table.shape[0] * spq
        in_specs.append(pl.BlockSpec(memory_space=pl.ANY))
        out_specs.append(pl.BlockSpec((None, N_HEADS, HEAD_DIM, LANES),
                                      lambda g, c, pt: (jnp.minimum(c, side_steps - 1) // spq, 0, 0, 0)))
        out_shape.append(jax.ShapeDtypeStruct((page_table.shape[0], N_HEADS, HEAD_DIM, LANES), F32))
        operands.append(ck)
        prefetch = [page_table]
        scratch += [pltpu.VMEM((2, pps) + ck.shape[1:], F32), pltpu.SemaphoreType.DMA((2,))]
    return pl.pallas_call(
        functools.partial(_wkv_kernel, side_pages=pps, side_steps=side_steps, side_spq=spq),
        grid_spec=pltpu.PrefetchScalarGridSpec(
            num_scalar_prefetch=len(prefetch), grid=(L // lb, nc), in_specs=in_specs, out_specs=out_specs,
            scratch_shapes=scratch),
        out_shape=out_shape,
        compiler_params=_cparams(("parallel", "arbitrary")),
        name="rwkv_wkv",
    )(*prefetch, *operands)


def _rwkv(pb, prev_row, wkv0, consts, G, Tc, Tw, side=None):
    B, T, _ = pb.shape
    L = B * N_HEADS
    big = T % LANE_CHUNK == 0
    if side is not None and not (big and _side_pages_per_step(side[1], T // Tw, Tw)):
        side = None
    r, w, k, v, a, b, g, bonus = _rwkv_prep(pb, prev_row[:, None, :], consts, G, Tc, big)
    if big:
        coef = [_to_lanes(x, Tw) for x in (r, w, k, v, a, b)]
    else:
        assert Tw == T
        coef = jnp.stack([r, w, k, v, a, b]).reshape(6, B, T, HEAD_DIM, N_HEADS)
        coef = jnp.transpose(coef, (0, 3, 2, 1, 4)).reshape(6, 1, HEAD_DIM, T, L)
    s0 = jnp.transpose(wkv0, (3, 2, 0, 1)).reshape(HEAD_DIM, HEAD_DIM, L)
    y, sfin, *side_out = _wkv(*coef, s0, side)
    if big:
        y = _from_lanes(y.reshape(T * HEAD_DIM, L), B)
    else:
        y = jnp.transpose(y.reshape(T, HEAD_DIM, B, N_HEADS), (2, 0, 1, 3)).reshape(B * T, WIDTH)
    sfin = jnp.transpose(sfin.reshape(HEAD_DIM, HEAD_DIM, B, N_HEADS), (2, 3, 1, 0))
    return y, g, bonus, sfin, (side_out[0] if side_out else None)


def _merge_kernel(x_ref, at_ref, y_ref, bonus_ref, g_ref, sg_ref, lng_ref, lnb_ref,
                  wa_ref, wb_ref, wo_ref, h_ref, *, y_transposed):
    D = x_ref.shape[1]
    y = y_ref[...].T if y_transposed else y_ref[...]
    d = y - _head_sum(y) * (1.0 / HEAD_DIM)
    var = _head_sum(d * d) * (1.0 / HEAD_DIM)
    yn = d * lax.rsqrt(var + GN_EPS) * lng_ref[...] + lnb_ref[...]
    ob = (yn + bonus_ref[...]) * g_ref[...]
    y_a = jnp.dot(at_ref[...].astype(BF16), wa_ref[...], preferred_element_type=F32)
    y_b = jnp.dot(ob.astype(BF16), wb_ref[...], preferred_element_type=F32)
    mix = sg_ref[:, 0:D].astype(F32) * y_a + sg_ref[:, D:2 * D].astype(F32) * y_b
    h_ref[...] = x_ref[...] + jnp.dot(mix.astype(BF16), wo_ref[...], preferred_element_type=F32)


def _merge(x, attn, y, bonus, g, sg, lng, lnb, wa, wb, wo, tm, y_transposed):
    B, T, D = x.shape
    row = lambda w: pl.BlockSpec((None, tm, w), lambda b, t: (b, t, 0))
    yspec = pl.BlockSpec((None, WIDTH, tm), lambda b, t: (b, 0, t)) if y_transposed else row(WIDTH)
    consts = (lng, lnb, wa, wb, wo)
    return pl.pallas_call(
        functools.partial(_merge_kernel, y_transposed=y_transposed),
        grid=(B, T // tm),
        in_specs=[row(D), row(WIDTH), yspec, row(WIDTH), row(WIDTH), row(2 * D)]
                 + [_const_spec(a.shape) for a in consts],
        out_specs=row(D),
        out_shape=jax.ShapeDtypeStruct((B, T, D), F32),
        compiler_params=_cparams(("parallel", "parallel")),
        name="merge",
    )(x, attn, y, bonus, g, sg, *consts)


FFN_CHUNK = 256


def _ffn_kernel(h_ref, cp_ref, ln2_ref, wup_ref, cw_ref, cb_ref, wdn_ref, lnf_ref, y_ref, cs_ref, carry, act_ref,
                *, dff):
    c = pl.program_id(1)

    @pl.when(c == 0)
    def _():
        carry[...] = cp_ref[...]

    h3 = h_ref[...]
    G, Tc, D = h3.shape
    h = h3.reshape(G * Tc, D)
    xb = _rms(h, ln2_ref[...]).astype(BF16)
    thead = lax.broadcasted_iota(jnp.int32, (1, SUBLANES, 1), 1)

    def conv(cols):
        u = jnp.dot(xb, wup_ref[:, cols], preferred_element_type=F32).reshape(G, Tc, FFN_CHUNK)
        w0, w1, w2, cb = cw_ref[0:1, cols], cw_ref[1:2, cols], cw_ref[2:3, cols], cb_ref[:, cols]
        r1 = pltpu.roll(u, 1, axis=1)
        r2 = pltpu.roll(u, 2, axis=1)
        p0 = carry[:, 0:1, cols]
        p1 = carry[:, 1:2, cols]
        h1 = jnp.where(thead == 0, p1, r1[:, 0:SUBLANES, :])
        h2 = jnp.where(thead == 0, p0, jnp.where(thead == 1, p1, r2[:, 0:SUBLANES, :]))
        cv = cb + h2 * w0 + h1 * w1 + u[:, 0:SUBLANES, :] * w2
        if Tc > SUBLANES:
            rest = slice(SUBLANES, Tc)
            cv = jnp.concatenate([cv, cb + r2[:, rest, :] * w0 + r1[:, rest, :] * w1 + u[:, rest, :] * w2], axis=1)
        carry[:, :, cols] = u[:, Tc - (CONV_W - 1):Tc, :]
        return cv.reshape(G * Tc, FFN_CHUNK)

    for n in range(dff // FFN_CHUNK):
        val = conv(slice(n * FFN_CHUNK, (n + 1) * FFN_CHUNK))
        gt = conv(slice(dff + n * FFN_CHUNK, dff + (n + 1) * FFN_CHUNK))
        act_ref[:, n * FFN_CHUNK:(n + 1) * FFN_CHUNK] = (gt * _sigmoid(gt) * val).astype(BF16)
    cs_ref[...] = carry[...]
    out = h + jnp.dot(act_ref[...], wdn_ref[...], preferred_element_type=F32)
    y_ref[...] = _rms(out, lnf_ref[...]).reshape(G, Tc, D)


def _ffn(h, conv_prev, ln2, wup, cw, cb, wdn, lnf, G, Tc):
    B, T, D = h.shape
    dff = wdn.shape[0]
    assert dff % FFN_CHUNK == 0 and Tc >= CONV_W - 1 and (G == 1 or Tc == T)
    consts_a = (ln2, wup, cw, cb, wdn, lnf)
    tile = pl.BlockSpec((G, Tc, D), lambda b, c: (b, c, 0))
    cst = pl.BlockSpec((G, CONV_W - 1, 2 * dff), lambda b, c: (b, 0, 0))
    return pl.pallas_call(
        functools.partial(_ffn_kernel, dff=dff),
        grid=(B // G, T // Tc),
        in_specs=[tile, cst] + [_const_spec(a.shape) for a in consts_a],
        out_specs=[tile, cst],
        out_shape=[jax.ShapeDtypeStruct((B, T, D), F32), jax.ShapeDtypeStruct((B, CONV_W - 1, 2 * dff), F32)],
        scratch_shapes=[pltpu.VMEM((G, CONV_W - 1, 2 * dff), F32), pltpu.VMEM((G * Tc, dff), BF16)],
        compiler_params=_cparams(("parallel", "arbitrary")),
        name="convffn",
    )(h, conv_prev, *consts_a)


def _trunk(x, attend, shift_prev, wkv0, conv_prev, p, tiles):
    B, T, D = x.shape
    Bv, Tv = tiles["rows"]
    q, kt, vt, pb, sg = _inproj(x.reshape(Bv, Tv, D), p["ln1_g"], p["wq"], p["wkvt"], p["wpb"], p["wg"], tiles["tm"])
    attn = attend(q, kt, vt).reshape(Bv, Tv, WIDTH)
    pb = pb.reshape(B, T, RWKV_PROJ)
    y, g, bonus, sfin, side_out = _rwkv(pb, _shift_cols(shift_prev, _to_jh), wkv0, p["rwkv_consts"],
                                        tiles["G"], tiles["Tc"], tiles["Tw"], tiles.get("side"))
    y_transposed = y.ndim == 3
    if not y_transposed:
        y = y.reshape(Bv, Tv, WIDTH)
    h = _merge(x.reshape(Bv, Tv, D), attn, y, bonus.reshape(Bv, Tv, WIDTH), g.reshape(Bv, Tv, WIDTH), sg,
               p["lnx_g"], p["lnx_b"], p["w_br_a"], p["w_br_b"], p["w_o"], tiles["tm"], y_transposed)
    yout, conv_new = _ffn(h.reshape(B, T, D), conv_prev, p["ln2_g"], p["w_up"], p["conv_w"], p["conv_b"],
                          p["w_down"], p["lnf_g"], tiles["G"], tiles["Tc"])
    return yout, kt, vt, sfin, _shift_cols(pb[:, -1, :], _from_jh), conv_new, side_out


def kernel(x_prompt, x_sample, cache_k, cache_v, page_table, state_wkv, state_tshift, state_conv, ln1_g, w_in, w_br_a, mu, w0, w2, a0, a2, g2, k_k, k_a, r_k, lnx_g, lnx_b, w_br_b, w_o, ln2_g, w_up, conv_w, conv_b, w_down, lnf_g):
    assert w_in.shape[0] == 1, "single-layer trunk"
    BP, S, D = x_prompt.shape
    DB, TS, _ = x_sample.shape
    zrows = jnp.zeros((D_AAA_LORA, WIDTH), F32)
    wpb = _shift_cols(w_in[0, :, 3 * WIDTH:3 * WIDTH + RWKV_PROJ], _to_jh)
    p = dict(
        ln1_g=ln1_g, ln2_g=ln2_g, lnf_g=lnf_g[None, :], lnx_g=_to_jh(lnx_g), lnx_b=_to_jh(lnx_b),
        wq=w_in[0, :, 0:WIDTH].astype(BF16),
        wkvt=jnp.transpose(w_in[0, :, WIDTH:3 * WIDTH]).astype(BF16),
        wpb=wpb.astype(BF16),
        wg=w_in[0, :, 3 * WIDTH + RWKV_PROJ:].astype(BF16),
        w_br_a=w_br_a[0].astype(BF16), w_br_b=_to_jh(w_br_b[0].T).T.astype(BF16), w_o=w_o[0].astype(BF16),
        w_up=w_up[0].astype(BF16), w_down=w_down[0].astype(BF16), conv_w=conv_w[0], conv_b=conv_b,
    )
    p["rwkv_consts"] = (_shift_cols(mu, _to_jh), _to_jh(w0), _to_jh(jnp.concatenate([w2[0], zrows], axis=0)),
                        _to_jh(a0), _to_jh(jnp.concatenate([zrows, a2[0]], axis=0)), _to_jh(g2[0]),
                        _to_jh(k_k), _to_jh(k_a), _to_jh(r_k.reshape(1, WIDTH)))

    ck = jnp.transpose(cache_k[0], (0, 2, 3, 1))
    cv = jnp.transpose(cache_v[0], (0, 2, 3, 1))

    tiles_p = dict(rows=(BP, S), tm=min(512, S), G=1, Tc=min(512, S), Tw=min(16, S), side=(ck, page_table))
    yp, ktp, vtp, wkv_p, tshift_p, conv_p, km = _trunk(
        x_prompt, lambda q, kt, vt: _moba_prompt(q, kt, vt),
        jnp.zeros((BP, RWKV_PROJ), F32), jnp.zeros((BP, N_HEADS, HEAD_DIM, HEAD_DIM), F32),
        jnp.zeros((BP, CONV_W - 1, w_up.shape[2]), F32), p, tiles_p)
    if km is None:
        km = _kmean(ck, page_table)

    tiles_s = dict(rows=(1, DB * TS), tm=DB * TS, G=DB, Tc=TS, Tw=TS)
    attend_s = lambda q, kt, vt: _moba_sample(q, kt, vt, ck, cv, page_table, km)
    ys, kts, vts, wkv_s, tshift_s, conv_s, _ = _trunk(
        x_sample, attend_s, state_tshift[0], state_wkv[0], state_conv[0], p, tiles_s)

    heads_p = lambda t: jnp.transpose(t.reshape(BP, N_HEADS, HEAD_DIM, S), (0, 3, 1, 2))[None]
    heads_s = lambda t: jnp.transpose(t.reshape(N_HEADS, HEAD_DIM, DB, TS), (2, 3, 0, 1))[None]
    return (yp, ys, heads_p(ktp), heads_p(vtp), wkv_p[None], tshift_p[None], conv_p[None],
            heads_s(kts), heads_s(vts), wkv_s[None], tshift_s[None], conv_s[None])
```

```python
import functools

import jax
import jax.numpy as jnp
from jax import lax
from jax.experimental import pallas as pl
from jax.experimental.pallas import tpu as pltpu

F32 = jnp.float32
BF16 = jnp.bfloat16
HIGHEST = lax.Precision.HIGHEST
NT_DIMS = (((1,), (1,)), ((), ()))

HEAD_DIM = 64
N_HEADS = 8
WIDTH = N_HEADS * HEAD_DIM
MOBA_BLOCK = 256
MOBA_TOPK = 3
PAGE_SIZE = 128
PAGES_PER_BLOCK = MOBA_BLOCK // PAGE_SIZE
D_DECAY_LORA = 64
D_AAA_LORA = 64
D_GATE_LORA = 128
RWKV_PROJ = 3 * WIDTH + D_DECAY_LORA + D_AAA_LORA + D_GATE_LORA
CONV_W = 3
RMS_EPS = 1e-6
GN_EPS = 64e-5
NEG = -1e30
ATTN_SCALE = HEAD_DIM ** -0.5

LANES = 128
SUBLANES = 8
VMEM_LIMIT = 56 * 1024 * 1024


def _cparams(sem):
    return pltpu.CompilerParams(dimension_semantics=sem, vmem_limit_bytes=VMEM_LIMIT)


def _const_spec(shape):
    nd = len(shape)
    return pl.BlockSpec(shape, lambda *_: (0,) * nd, pipeline_mode=pl.Buffered(1))


def _sigmoid(x):
    return 1.0 / (1.0 + jnp.exp(-x))


def _rms(x, g):
    return x * lax.rsqrt(jnp.mean(x * x, axis=-1, keepdims=True) + RMS_EPS) * g


def _inproj_kernel(x_ref, g_ref, wq_ref, wkvt_ref, wpb_ref, wg_ref,
                   q_ref, kt_ref, vt_ref, pb_ref, sg_ref):
    xb = _rms(x_ref[...], g_ref[...]).astype(BF16)
    q_ref[...] = jnp.dot(xb, wq_ref[...], preferred_element_type=F32)
    kt_ref[...] = lax.dot_general(wkvt_ref[0:WIDTH, :], xb, NT_DIMS, preferred_element_type=F32)
    vt_ref[...] = lax.dot_general(wkvt_ref[WIDTH:2 * WIDTH, :], xb, NT_DIMS, preferred_element_type=F32)
    pb_ref[...] = jnp.dot(xb, wpb_ref[...], preferred_element_type=F32)
    sg_ref[...] = _sigmoid(jnp.dot(xb, wg_ref[...], preferred_element_type=F32)).astype(sg_ref.dtype)


def _inproj(x, ln_g, wq, wkvt, wpb, wg, tm):
    B, T, D = x.shape
    row = lambda w: pl.BlockSpec((None, tm, w), lambda b, t: (b, t, 0))
    col = pl.BlockSpec((None, WIDTH, tm), lambda b, t: (b, 0, t))
    return pl.pallas_call(
        _inproj_kernel,
        grid=(B, T // tm),
        in_specs=[row(D), _const_spec((1, D)), _const_spec(wq.shape), _const_spec(wkvt.shape),
                  _const_spec(wpb.shape), _const_spec(wg.shape)],
        out_specs=[row(WIDTH), col, col, row(RWKV_PROJ), row(2 * D)],
        out_shape=[jax.ShapeDtypeStruct((B, T, WIDTH), F32),
                   jax.ShapeDtypeStruct((B, WIDTH, T), F32),
                   jax.ShapeDtypeStruct((B, WIDTH, T), F32),
                   jax.ShapeDtypeStruct((B, T, RWKV_PROJ), F32),
                   jax.ShapeDtypeStruct((B, T, 2 * D), BF16)],
        compiler_params=_cparams(("parallel", "parallel")),
        name="inproj",
    )(x, ln_g, wq, wkvt, wpb, wg)


def _moba_prompt_kernel(q_ref, kt_ref, vt_ref, o_ref, kp_ref, vtb_ref, qa_ref, s_ref, p_ref, *, nblk):
    S = q_ref.shape[0]
    kt = kt_ref[...]
    vtb_ref[...] = vt_ref[...].astype(BF16)
    lane = lax.broadcasted_iota(jnp.int32, (1, LANES), 1)
    rowd = lax.broadcasted_iota(jnp.int32, (LANES, 1), 0)
    keyblk = lax.broadcasted_iota(jnp.int32, (1, S), 1) // MOBA_BLOCK
    km = jnp.zeros((LANES, LANES), F32)
    first = rowd < HEAD_DIM
    for j in range(nblk):
        colj = jnp.sum(kt[:, j * MOBA_BLOCK:(j + 1) * MOBA_BLOCK], axis=1, keepdims=True) * (1.0 / MOBA_BLOCK)
        km = jnp.where(((lane == j) & jnp.logical_not(first)) | ((lane == HEAD_DIM + j) & first), colj, km)
    qrow = lax.broadcasted_iota(jnp.int32, (MOBA_BLOCK, 1), 0)
    kcol = lax.broadcasted_iota(jnp.int32, (1, MOBA_BLOCK), 1)
    for hh in range(2):
        in_r = (rowd >= HEAD_DIM * hh) & (rowd < HEAD_DIM * (hh + 1))
        ind = jnp.where(rowd - HEAD_DIM * (1 - hh) == keyblk, 1.0, 0.0)
        kp_ref[hh] = jnp.where(in_r, kt, ind).astype(BF16)
    def select(i):
        rows = slice(i * MOBA_BLOCK, (i + 1) * MOBA_BLOCK)
        qi = q_ref[rows, :]
        if i > 0:
            gate = jnp.dot(qi, km, precision=HIGHEST, preferred_element_type=F32)
        for hh in range(2):
            in_l = (lane >= HEAD_DIM * hh) & (lane < HEAD_DIM * (hh + 1))
            base = HEAD_DIM * (1 - hh)
            jl = lane - base
            is_bias_lane = (jl >= 0) & (jl < nblk)
            if i > 0:
                rank = jnp.zeros((MOBA_BLOCK, LANES), F32)
                for jp in range(i):
                    cj = gate[:, base + jp:base + jp + 1]
                    beats = (cj > gate) | ((cj == gate) & (jp < jl))
                    rank = rank + jnp.where(beats, 1.0, 0.0)
                keep = ((jl < i) & (rank < min(MOBA_TOPK, nblk - 1))) | (jl == i)
            else:
                keep = jl == 0
            bias = jnp.where(is_bias_lane & jnp.logical_not(keep), NEG, 0.0)
            qa_ref[hh, rows, :] = jnp.where(in_l, qi * ATTN_SCALE, bias).astype(BF16)
    def logits(i):
        rows = slice(i * MOBA_BLOCK, (i + 1) * MOBA_BLOCK)
        m = []
        for hh in range(2):
            qa = qa_ref[hh, rows, :]
            mx = None
            for j in range(i + 1):
                keys = slice(j * MOBA_BLOCK, (j + 1) * MOBA_BLOCK)
                s = jnp.dot(qa, kp_ref[hh, :, keys], preferred_element_type=F32)
                if j == i:
                    s = jnp.where(kcol <= qrow, s, NEG)
                s_ref[i % 2, hh, :, keys] = s
                e = jnp.maximum(s[:, 0:LANES], s[:, LANES:2 * LANES])
                mx = e if mx is None else jnp.maximum(mx, e)
            m.append(jnp.max(mx, axis=-1, keepdims=True))
        return m

    def softmax(i, m):
        l = []
        for hh in range(2):
            ls = None
            for j in range(i + 1):
                keys = slice(j * MOBA_BLOCK, (j + 1) * MOBA_BLOCK)
                p = jnp.exp(s_ref[i % 2, hh, :, keys] - m[hh])
                p_ref[i % 2, hh, :, keys] = p.astype(BF16)
                e = p[:, 0:LANES] + p[:, LANES:2 * LANES]
                ls = e if ls is None else ls + e
            l.append(jnp.sum(ls, axis=-1, keepdims=True))
        return l

    def pv(i, l):
        rows = slice(i * MOBA_BLOCK, (i + 1) * MOBA_BLOCK)
        nk = (i + 1) * MOBA_BLOCK
        outs = [lax.dot_general(p_ref[i % 2, hh, :, 0:nk], vtb_ref[:, 0:nk], NT_DIMS,
                                preferred_element_type=F32) / l[hh] for hh in range(2)]
        o_ref[rows, :] = jnp.where(lane < HEAD_DIM, outs[0], outs[1]).astype(o_ref.dtype)

    for i in range(nblk):
        select(i)
    m_next = logits(0)
    for i in range(nblk):
        m_cur = m_next
        if i + 1 < nblk:
            m_next = logits(i + 1)
        pv(i, softmax(i, m_cur))


def _moba_prompt(q, kt, vt):
    B, S, _ = q.shape
    nblk = S // MOBA_BLOCK
    qspec = pl.BlockSpec((None, S, LANES), lambda b, h: (b, 0, h))
    tspec = pl.BlockSpec((None, LANES, S), lambda b, h: (b, h, 0))
    return pl.pallas_call(
        functools.partial(_moba_prompt_kernel, nblk=nblk),
        grid=(B, WIDTH // LANES),
        in_specs=[qspec, tspec, tspec],
        out_specs=qspec,
        out_shape=jax.ShapeDtypeStruct((B, S, WIDTH), BF16),
        scratch_shapes=[pltpu.VMEM((2, LANES, S), BF16), pltpu.VMEM((LANES, S), BF16), pltpu.VMEM((2, S, LANES), BF16),
                        pltpu.VMEM((2, 2, MOBA_BLOCK, S), F32), pltpu.VMEM((2, 2, MOBA_BLOCK, S), BF16)],
        compiler_params=_cparams(("parallel", "parallel")),
        name="moba_prompt",
    )(q, kt, vt)


def _kmean_accumulate(pages, km_ref, chunk):
    @pl.when(chunk == 0)
    def _():
        km_ref[...] = jnp.zeros(km_ref.shape, F32)

    lane = lax.broadcasted_iota(jnp.int32, (1, 1, LANES), 2)
    blocks_per_step = len(pages) // PAGES_PER_BLOCK
    for jj in range(blocks_per_step):
        tot = pages[PAGES_PER_BLOCK * jj][...]
        for r in range(1, PAGES_PER_BLOCK):
            tot = tot + pages[PAGES_PER_BLOCK * jj + r][...]
        colj = jnp.sum(tot, axis=-1, keepdims=True) * (1.0 / MOBA_BLOCK)
        km_ref[...] = jnp.where(lane == chunk * blocks_per_step + jj, colj, km_ref[...])


def _kmean_specs(n_pages, pages_per_step, step_of):
    spq = n_pages // pages_per_step

    def page_spec(r):
        def index(*ids):
            step, pt = step_of(*ids[:-1]), ids[-1]
            return (pt[step // spq, (step % spq) * pages_per_step + r], 0, 0, 0)
        return pl.BlockSpec((None, N_HEADS, HEAD_DIM, PAGE_SIZE), index)

    km_spec = pl.BlockSpec((None, N_HEADS, HEAD_DIM, LANES), lambda *ids: (step_of(*ids[:-1]) // spq, 0, 0, 0))
    return [page_spec(r) for r in range(pages_per_step)], km_spec


def _kmean_kernel(pt_ref, *refs):
    del pt_ref
    _kmean_accumulate(refs[:-1], refs[-1], pl.program_id(1))


def _kmean(ck, page_table, pages_per_step=32):
    DB, n_pages = page_table.shape
    pages_per_step = min(pages_per_step, n_pages)
    assert n_pages // PAGES_PER_BLOCK <= LANES and n_pages % pages_per_step == 0
    spq = n_pages // pages_per_step
    page_specs, km_spec = _kmean_specs(n_pages, pages_per_step, lambda b, c: b * spq + c)
    return pl.pallas_call(
        _kmean_kernel,
        grid_spec=pltpu.PrefetchScalarGridSpec(
            num_scalar_prefetch=1, grid=(DB, spq), in_specs=page_specs, out_specs=km_spec),
        out_shape=jax.ShapeDtypeStruct((DB, N_HEADS, HEAD_DIM, LANES), F32),
        compiler_params=_cparams(("parallel", "arbitrary")),
        name="moba_kmean",
    )(page_table, *([ck] * pages_per_step))


def _select_kernel(q_ref, km_ref, sel_ref, *, nblk):
    lane = lax.broadcasted_iota(jnp.int32, (1, LANES), 1)
    lanef = lane.astype(F32)
    for h in range(N_HEADS):
        gate = jnp.dot(q_ref[h], km_ref[h], precision=HIGHEST, preferred_element_type=F32)
        gate = jnp.where(lane < nblk, gate, -jnp.inf)
        out = jnp.zeros(gate.shape, F32)
        for r in range(MOBA_TOPK):
            m = jnp.max(gate, axis=-1, keepdims=True)
            idx = jnp.min(jnp.where(gate == m, lanef, float(LANES)), axis=-1, keepdims=True)
            out = jnp.where(lane == r, idx, out)
            gate = jnp.where(lanef == idx, -jnp.inf, gate)
        sel_ref[h] = out.astype(jnp.int32)


def _select(qh, km, nblk):
    DB, H, T, Dh = qh.shape
    return pl.pallas_call(
        functools.partial(_select_kernel, nblk=nblk),
        grid=(DB,),
        in_specs=[pl.BlockSpec((None, H, T, Dh), lambda b: (b, 0, 0, 0)),
                  pl.BlockSpec((None, H, Dh, LANES), lambda b: (b, 0, 0, 0))],
        out_specs=pl.BlockSpec((None, H, T, LANES), lambda b: (b, 0, 0, 0)),
        out_shape=jax.ShapeDtypeStruct((DB, H, T, LANES), jnp.int32),
        compiler_params=_cparams(("parallel",)),
        name="moba_select",
    )(qh, km)


SAMPLE_HEADS_PER_STEP = 1


def _sample_attn_kernel(sel_ref, pt_ref, qt_ref, knt_ref, vnt_ref, ck_hbm, cv_hbm, o_ref, kbuf, vbuf, sems,
                        *, T, n_sel, n_heads):
    b, hg = pl.program_id(0), pl.program_id(1)
    hp = SAMPLE_HEADS_PER_STEP
    ng = n_heads // hp
    step = b * ng + hg
    nstep = pl.num_programs(0) * ng

    def copies(bb, gg, slot, lookup):
        out = []
        for e in range(hp):
            hh = gg * hp + e
            for t in range(T):
                for s in range(n_sel):
                    blk = sel_ref[((bb * n_heads + hh) * T + t) * n_sel + s] if lookup else 0
                    for r in range(PAGES_PER_BLOCK):
                        page = pt_ref[bb, blk * PAGES_PER_BLOCK + r] if lookup else 0
                        dst = pl.ds((s * PAGES_PER_BLOCK + r) * PAGE_SIZE, PAGE_SIZE)
                        out.append(pltpu.make_async_copy(ck_hbm.at[page, hh], kbuf.at[slot, e, t, :, dst],
                                                         sems.at[slot]))
                        out.append(pltpu.make_async_copy(cv_hbm.at[page, hh], vbuf.at[slot, e, t, :, dst],
                                                         sems.at[slot]))
        return out

    @pl.when(step == 0)
    def _():
        for n, cp in enumerate(copies(b, hg, 0, True)):
            cp.start(priority=n % 2)

    @pl.when(step + 1 < nstep)
    def _():
        nxt = step + 1
        for n, cp in enumerate(copies(nxt // ng, nxt % ng, nxt % 2, True)):
            cp.start(priority=n % 2)

    slot = step % 2
    for cp in copies(0, 0, slot, False):
        cp.wait()

    lane_t = lax.broadcasted_iota(jnp.int32, (1, T), 1)
    row_t = lax.broadcasted_iota(jnp.int32, (T, 1), 0)
    lane = lax.broadcasted_iota(jnp.int32, (1, LANES), 1)
    for e in range(hp):
        qt = qt_ref[e]
        knt = knt_ref[e]
        vnt = vnt_ref[e]
        s_past, s_new = [], []
        for t in range(T):
            qc = qt[:, t:t + 1]
            s_past.append(jnp.sum(kbuf[slot, e, t] * qc, axis=0, keepdims=True))
            s_new.append(jnp.sum(knt * qc, axis=0, keepdims=True))
        s_past = jnp.concatenate(s_past, axis=0) * ATTN_SCALE
        s_new = jnp.concatenate(s_new, axis=0) * ATTN_SCALE
        s_new = jnp.where(lane_t <= row_t, s_new, NEG)
        m = jnp.maximum(jnp.max(s_past, axis=-1, keepdims=True), jnp.max(s_new, axis=-1, keepdims=True))
        p_past = jnp.exp(s_past - m)
        p_new = jnp.exp(s_new - m)
        l = jnp.sum(p_past, axis=-1, keepdims=True) + jnp.sum(p_new, axis=-1, keepdims=True)
        out = jnp.zeros((HEAD_DIM, LANES), F32)
        for t in range(T):
            o = (jnp.sum(vbuf[slot, e, t] * p_past[t:t + 1, :], axis=-1, keepdims=True)
                 + jnp.sum(vnt * p_new[t:t + 1, :], axis=-1, keepdims=True)) / l[t:t + 1, :]
            out = jnp.where(lane == t, o, out)
        o_ref[e] = out


def _sample_attn(sel_flat, page_table, qt, knt, vnt, ck, cv, n_sel):
    DB, H, Dh, T = qt.shape
    hp = SAMPLE_HEADS_PER_STEP
    assert H % hp == 0
    small = pl.BlockSpec((None, hp, Dh, T), lambda b, g, *_: (b, g, 0, 0))
    nkeys = n_sel * MOBA_BLOCK
    return pl.pallas_call(
        functools.partial(_sample_attn_kernel, T=T, n_sel=n_sel, n_heads=H),
        grid_spec=pltpu.PrefetchScalarGridSpec(
            num_scalar_prefetch=2,
            grid=(DB, H // hp),
            in_specs=[small, small, small, pl.BlockSpec(memory_space=pl.ANY), pl.BlockSpec(memory_space=pl.ANY)],
            out_specs=pl.BlockSpec((None, hp, Dh, LANES), lambda b, g, *_: (b, g, 0, 0)),
            scratch_shapes=[pltpu.VMEM((2, hp, T, Dh, nkeys), F32), pltpu.VMEM((2, hp, T, Dh, nkeys), F32),
                            pltpu.SemaphoreType.DMA((2,))],
        ),
        out_shape=jax.ShapeDtypeStruct((DB, H, Dh, LANES), F32),
        compiler_params=_cparams(("arbitrary", "arbitrary")),
        name="moba_sample",
    )(sel_flat, page_table, qt, knt, vnt, ck, cv)


def _moba_sample(q, kt, vt, ck, cv, page_table, km):
    DB, n_pages = page_table.shape
    T = q.shape[1] // DB
    nblk = n_pages // PAGES_PER_BLOCK
    n_sel = min(MOBA_TOPK, nblk)
    assert n_pages % PAGES_PER_BLOCK == 0 and n_sel == MOBA_TOPK
    qh = jnp.transpose(q.reshape(DB, T, N_HEADS, HEAD_DIM), (0, 2, 1, 3))
    qt = jnp.transpose(qh, (0, 1, 3, 2))
    knt = jnp.transpose(kt.reshape(N_HEADS, HEAD_DIM, DB, T), (2, 0, 1, 3))
    vnt = jnp.transpose(vt.reshape(N_HEADS, HEAD_DIM, DB, T), (2, 0, 1, 3))
    sel = _select(qh, km, nblk)[..., :n_sel].reshape(-1)
    o = _sample_attn(sel, page_table, qt, knt, vnt, ck, cv, n_sel)
    return jnp.transpose(o[..., :T], (0, 3, 1, 2)).reshape(DB * T, WIDTH)


def _to_jh(x):
    lead = x.shape[:-1]
    return jnp.swapaxes(x.reshape(*lead, N_HEADS, HEAD_DIM), -1, -2).reshape(*lead, WIDTH)


def _from_jh(x):
    lead = x.shape[:-1]
    return jnp.swapaxes(x.reshape(*lead, HEAD_DIM, N_HEADS), -1, -2).reshape(*lead, WIDTH)


def _shift_cols(x, reorder):
    parts = [reorder(x[..., i * WIDTH:(i + 1) * WIDTH]) for i in range(3)]
    return jnp.concatenate(parts + [x[..., 3 * WIDTH:]], axis=-1)


def _head_sum(x):
    z = x[:, 0:LANES]
    for p in range(1, x.shape[1] // LANES):
        z = z + x[:, p * LANES:(p + 1) * LANES]
    shift = N_HEADS
    while shift < LANES:
        z = z + pltpu.roll(z, shift, axis=1)
        shift *= 2
    return jnp.concatenate([z] * (x.shape[1] // LANES), axis=1)


def _rwkv_prep_kernel(pb_ref, prev_ref, mu_ref, w0_ref, w2p_ref, a0_ref, a2p_ref, g2_ref, kk_ref, ka_ref,
                      rk_ref, r_ref, w_ref, k_ref, v_ref, a_ref, b_ref, g_ref, bonus_ref, carry, *, transposed):
    c = pl.program_id(1)

    @pl.when(c == 0)
    def _():
        carry[...] = prev_ref[...]

    pb3 = pb_ref[...]
    G, Tc, P = pb3.shape
    tpos = lax.broadcasted_iota(jnp.int32, (1, Tc, 1), 1)
    prev3 = jnp.where(tpos == 0, carry[...], pltpu.roll(pb3, 1, axis=1))
    carry[...] = pb3[:, Tc - 1:Tc, :]
    pb = pb3.reshape(G * Tc, P)
    xm = pb + (prev3.reshape(G * Tc, P) - pb) * mu_ref[...]
    r = xm[:, 0:WIDTH]
    k = xm[:, WIDTH:2 * WIDTH]
    v = xm[:, 2 * WIDTH:3 * WIDTH]
    wa = xm[:, 3 * WIDTH:3 * WIDTH + D_DECAY_LORA + D_AAA_LORA]
    gd = xm[:, 3 * WIDTH + D_DECAY_LORA + D_AAA_LORA:]
    z = w0_ref[...] + jnp.dot(jnp.tanh(wa), w2p_ref[...], precision=HIGHEST, preferred_element_type=F32)
    softplus_negz = jnp.maximum(-z, 0.0) + jnp.log(1.0 + jnp.exp(-jnp.abs(z)))
    w = jnp.exp(-jnp.exp(-softplus_negz - 0.5))
    lr = _sigmoid(a0_ref[...] + jnp.dot(wa, a2p_ref[...], precision=HIGHEST, preferred_element_type=F32))
    g_ref[...] = jnp.dot(_sigmoid(gd), g2_ref[...], precision=HIGHEST, preferred_element_type=F32)
    kk = k * kk_ref[...]
    kk = kk / jnp.maximum(jnp.sqrt(_head_sum(kk * kk)), 1e-12)
    k2 = k * (1.0 + (lr - 1.0) * ka_ref[...])
    bonus_ref[...] = _head_sum(r * k2 * rk_ref[...]) * v
    outs = ((r_ref, r), (w_ref, w), (k_ref, k2), (v_ref, v), (a_ref, -kk), (b_ref, kk * lr))
    for ref, val in outs:
        ref[...] = val.T if transposed else val


def _rwkv_prep(pb, prev_row, consts, G, Tc, transposed):
    B, T, P = pb.shape
    nt = T // Tc
    assert G == 1 or nt == 1
    nat = jax.ShapeDtypeStruct((B * T, WIDTH), F32)
    nat_spec = pl.BlockSpec((G * Tc, WIDTH), lambda b, c: (b * nt + c, 0))
    if transposed:
        assert G == 1
        vec, vec_spec = jax.ShapeDtypeStruct((B, WIDTH, T), F32), pl.BlockSpec((None, WIDTH, Tc), lambda b, c: (b, 0, c))
    else:
        vec, vec_spec = nat, nat_spec
    return pl.pallas_call(
        functools.partial(_rwkv_prep_kernel, transposed=transposed),
        grid=(B // G, nt),
        in_specs=[pl.BlockSpec((G, Tc, P), lambda b, c: (b, c, 0)),
                  pl.BlockSpec((G, 1, P), lambda b, c: (b, 0, 0))] + [_const_spec(a.shape) for a in consts],
        out_specs=[vec_spec] * 6 + [nat_spec] * 2,
        out_shape=[vec] * 6 + [nat] * 2,
        scratch_shapes=[pltpu.VMEM((G, 1, P), F32)],
        compiler_params=_cparams(("parallel", "arbitrary")),
        name="rwkv_prep",
    )(pb, prev_row, *consts)


LANE_CHUNK = 128


def _to_lanes_kernel(x_ref, o_ref, *, tw):
    nb = x_ref.shape[0]
    for j in range(HEAD_DIM):
        m = x_ref[:, j * N_HEADS:(j + 1) * N_HEADS, :].reshape(nb * N_HEADS, LANE_CHUNK)
        mt = m.T
        for q in range(LANE_CHUNK // tw):
            o_ref[q, j] = mt[q * tw:(q + 1) * tw, :]


def _to_lanes(xt, tw):
    B, W, T = xt.shape
    L = B * N_HEADS
    assert L <= LANES and T % LANE_CHUNK == 0 and LANE_CHUNK % tw == 0
    return pl.pallas_call(
        functools.partial(_to_lanes_kernel, tw=tw),
        grid=(T // LANE_CHUNK,),
        in_specs=[pl.BlockSpec((B, W, LANE_CHUNK), lambda c: (0, 0, c))],
        out_specs=pl.BlockSpec((LANE_CHUNK // tw, HEAD_DIM, tw, L), lambda c: (c, 0, 0, 0)),
        out_shape=jax.ShapeDtypeStruct((T // tw, HEAD_DIM, tw, L), F32),
        compiler_params=_cparams(("parallel",)),
        name="to_lanes",
    )(xt)


def _from_lanes_kernel(y_ref, o_ref):
    nb = o_ref.shape[0]
    for i in range(HEAD_DIM):
        m = y_ref[pl.ds(i, LANE_CHUNK, stride=HEAD_DIM), :]
        o_ref[:, i * N_HEADS:(i + 1) * N_HEADS, :] = m.T.reshape(nb, N_HEADS, LANE_CHUNK)


def _from_lanes(y2d, B):
    TN, L = y2d.shape
    T = TN // HEAD_DIM
    assert L == B * N_HEADS <= LANES and T % LANE_CHUNK == 0
    return pl.pallas_call(
        _from_lanes_kernel,
        grid=(T // LANE_CHUNK,),
        in_specs=[pl.BlockSpec((LANE_CHUNK * HEAD_DIM, L), lambda c: (c, 0))],
        out_specs=pl.BlockSpec((B, WIDTH, LANE_CHUNK), lambda c: (0, 0, c)),
        out_shape=jax.ShapeDtypeStruct((B, WIDTH, T), F32),
        compiler_params=_cparams(("parallel",)),
        name="from_lanes",
    )(y2d)


WKV_UNROLL = 32


def _wkv_kernel(*refs, side_pages, side_steps, side_spq):
    if side_pages:
        pt_ref, refs = refs[0], refs[1:]
        r_ref, w_ref, k_ref, v_ref, a_ref, b_ref, s0_ref, ck_hbm, y_ref, sfin_ref, km_ref, S, pbuf, sems = refs
    else:
        r_ref, w_ref, k_ref, v_ref, a_ref, b_ref, s0_ref, y_ref, sfin_ref, S = refs
    c = pl.program_id(1)
    N, Tw = r_ref.shape[0], r_ref.shape[1]
    trips = N // WKV_UNROLL

    @pl.when(c == 0)
    def _():
        S[...] = s0_ref[...]

    if side_pages:
        def page_copies(s, lookup):
            seq, first = s // side_spq, (s % side_spq) * side_pages
            return [pltpu.make_async_copy(ck_hbm.at[pt_ref[seq, first + r] if lookup else 0],
                                          pbuf.at[s % 2, r], sems.at[s % 2]) for r in range(side_pages)]

        @pl.when(c == 0)
        def _():
            for cp in page_copies(c, True):
                cp.start()

        @pl.when(c + 1 < side_steps)
        def _():
            for cp in page_copies(c + 1, True):
                cp.start()

        @pl.when(c < side_steps)
        def _():
            for cp in page_copies(c, False):
                cp.wait()
            _kmean_accumulate([pbuf.at[c % 2, r] for r in range(side_pages)], km_ref, c % side_spq)

    def token(t, carry):
        row = lambda ref, j: ref[j, pl.ds(t, 1), :]

        def sa_body(jb, sa):
            for jj in range(WKV_UNROLL):
                j = jb * WKV_UNROLL + jj
                sa = sa + S[j] * row(a_ref, j)
            return sa

        sa = lax.fori_loop(0, trips, sa_body, jnp.zeros(S.shape[1:], F32))
        vt = v_ref[pl.ds(t, N, stride=Tw), :]

        def up_body(jb, y):
            for jj in range(WKV_UNROLL):
                j = jb * WKV_UNROLL + jj
                s = S[j] * row(w_ref, j) + sa * row(b_ref, j) + vt * row(k_ref, j)
                S[j] = s
                y = y + s * row(r_ref, j)
            return y

        y_ref[t] = lax.fori_loop(0, N // WKV_UNROLL, up_body, jnp.zeros(S.shape[1:], F32))
        return carry

    lax.fori_loop(0, Tw, token, 0)

    @pl.when(c == pl.num_programs(1) - 1)
    def _():
        sfin_ref[...] = S[...]


def _side_pages_per_step(page_table, n_steps, tokens_per_step):
    DB, n_pages = page_table.shape
    for pps in range(PAGES_PER_BLOCK, min(n_pages, tokens_per_step * PAGES_PER_BLOCK) + 1, PAGES_PER_BLOCK):
        if n_pages % pps == 0 and 2 <= DB * (n_pages // pps) <= n_steps:
            return pps
    return 0


def _wkv(r, w, k, v, a, b, s0, side=None):
    nc, N, Tw, L = r.shape
    lb = min(LANES, L)
    vec = pl.BlockSpec((None, N, Tw, lb), lambda g, c, *_: (c, 0, 0, g))
    st = pl.BlockSpec((N, N, lb), lambda g, c, *_: (0, 0, g))
    in_specs = [vec, vec, vec, pl.BlockSpec((N * Tw, lb), lambda g, c, *_: (c, g)), vec, vec, st]
    out_specs = [pl.BlockSpec((Tw, N, lb), lambda g, c, *_: (c, 0, g)), st]
    out_shape = [jax.ShapeDtypeStruct((nc * Tw, N, L), F32), jax.ShapeDtypeStruct((N, N, L), F32)]
    operands = [r, w, k, v.reshape(nc * N * Tw, L), a, b, s0]
    scratch = [pltpu.VMEM((N, N, lb), F32)]
    prefetch, pps, side_steps, spq = [], 0, 0, 1
    if side is not None:
        ck, page_table = side
        assert L == lb
        pps = _side_pages_per_step(page_table, nc, Tw)
        spq = page_table.shape[1] // pps
        side_steps = page_table.shape[0] * spq
        in_specs.append(pl.BlockSpec(memory_space=pl.ANY))
        out_specs.append(pl.BlockSpec((None, N_HEADS, HEAD_DIM, LANES),
                                      lambda g, c, pt: (jnp.minimum(c, side_steps - 1) // spq, 0, 0, 0)))
        out_shape.append(jax.ShapeDtypeStruct((page_table.shape[0], N_HEADS, HEAD_DIM, LANES), F32))
        operands.append(ck)
        prefetch = [page_table]
        scratch += [pltpu.VMEM((2, pps) + ck.shape[1:], F32), pltpu.SemaphoreType.DMA((2,))]
    return pl.pallas_call(
        functools.partial(_wkv_kernel, side_pages=pps, side_steps=side_steps, side_spq=spq),
        grid_spec=pltpu.PrefetchScalarGridSpec(
            num_scalar_prefetch=len(prefetch), grid=(L // lb, nc), in_specs=in_specs, out_specs=out_specs,
            scratch_shapes=scratch),
        out_shape=out_shape,
        compiler_params=_cparams(("parallel", "arbitrary")),
        name="rwkv_wkv",
    )(*prefetch, *operands)


def _rwkv(pb, prev_row, wkv0, consts, G, Tc, Tw, side=None):
    B, T, _ = pb.shape
    L = B * N_HEADS
    big = T % LANE_CHUNK == 0
    if side is not None and not (big and _side_pages_per_step(side[1], T // Tw, Tw)):
        side = None
    r, w, k, v, a, b, g, bonus = _rwkv_prep(pb, prev_row[:, None, :], consts, G, Tc, big)
    if big:
        coef = [_to_lanes(x, Tw) for x in (r, w, k, v, a, b)]
    else:
        assert Tw == T
        coef = jnp.stack([r, w, k, v, a, b]).reshape(6, B, T, HEAD_DIM, N_HEADS)
        coef = jnp.transpose(coef, (0, 3, 2, 1, 4)).reshape(6, 1, HEAD_DIM, T, L)
    s0 = jnp.transpose(wkv0, (3, 2, 0, 1)).reshape(HEAD_DIM, HEAD_DIM, L)
    y, sfin, *side_out = _wkv(*coef, s0, side)
    if big:
        y = _from_lanes(y.reshape(T * HEAD_DIM, L), B)
    else:
        y = jnp.transpose(y.reshape(T, HEAD_DIM, B, N_HEADS), (2, 0, 1, 3)).reshape(B * T, WIDTH)
    sfin = jnp.transpose(sfin.reshape(HEAD_DIM, HEAD_DIM, B, N_HEADS), (2, 3, 1, 0))
    return y, g, bonus, sfin, (side_out[0] if side_out else None)


def _merge_kernel(x_ref, at_ref, y_ref, bonus_ref, g_ref, sg_ref, lng_ref, lnb_ref,
                  wa_ref, wb_ref, wo_ref, h_ref, *, y_transposed):
    D = x_ref.shape[1]
    y = y_ref[...].T if y_transposed else y_ref[...]
    d = y - _head_sum(y) * (1.0 / HEAD_DIM)
    var = _head_sum(d * d) * (1.0 / HEAD_DIM)
    yn = d * lax.rsqrt(var + GN_EPS) * lng_ref[...] + lnb_ref[...]
    ob = (yn + bonus_ref[...]) * g_ref[...]
    y_a = jnp.dot(at_ref[...].astype(BF16), wa_ref[...], preferred_element_type=F32)
    y_b = jnp.dot(ob.astype(BF16), wb_ref[...], preferred_element_type=F32)
    mix = sg_ref[:, 0:D].astype(F32) * y_a + sg_ref[:, D:2 * D].astype(F32) * y_b
    h_ref[...] = x_ref[...] + jnp.dot(mix.astype(BF16), wo_ref[...], preferred_element_type=F32)


def _merge(x, attn, y, bonus, g, sg, lng, lnb, wa, wb, wo, tm, y_transposed):
    B, T, D = x.shape
    row = lambda w: pl.BlockSpec((None, tm, w), lambda b, t: (b, t, 0))
    yspec = pl.BlockSpec((None, WIDTH, tm), lambda b, t: (b, 0, t)) if y_transposed else row(WIDTH)
    consts = (lng, lnb, wa, wb, wo)
    return pl.pallas_call(
        functools.partial(_merge_kernel, y_transposed=y_transposed),
        grid=(B, T // tm),
        in_specs=[row(D), row(WIDTH), yspec, row(WIDTH), row(WIDTH), row(2 * D)]
                 + [_const_spec(a.shape) for a in consts],
        out_specs=row(D),
        out_shape=jax.ShapeDtypeStruct((B, T, D), F32),
        compiler_params=_cparams(("parallel", "parallel")),
        name="merge",
    )(x, attn, y, bonus, g, sg, *consts)


FFN_CHUNK = 256


def _ffn_kernel(h_ref, cp_ref, ln2_ref, wup_ref, cw_ref, cb_ref, wdn_ref, lnf_ref, y_ref, cs_ref, carry, act_ref,
                *, dff):
    c = pl.program_id(1)

    @pl.when(c == 0)
    def _():
        carry[...] = cp_ref[...]

    h3 = h_ref[...]
    G, Tc, D = h3.shape
    h = h3.reshape(G * Tc, D)
    xb = _rms(h, ln2_ref[...]).astype(BF16)
    thead = lax.broadcasted_iota(jnp.int32, (1, SUBLANES, 1), 1)

    def conv(cols):
        u = jnp.dot(xb, wup_ref[:, cols], preferred_element_type=F32).reshape(G, Tc, FFN_CHUNK)
        w0, w1, w2, cb = cw_ref[0:1, cols], cw_ref[1:2, cols], cw_ref[2:3, cols], cb_ref[:, cols]
        r1 = pltpu.roll(u, 1, axis=1)
        r2 = pltpu.roll(u, 2, axis=1)
        p0 = carry[:, 0:1, cols]
        p1 = carry[:, 1:2, cols]
        h1 = jnp.where(thead == 0, p1, r1[:, 0:SUBLANES, :])
        h2 = jnp.where(thead == 0, p0, jnp.where(thead == 1, p1, r2[:, 0:SUBLANES, :]))
        cv = cb + h2 * w0 + h1 * w1 + u[:, 0:SUBLANES, :] * w2
        if Tc > SUBLANES:
            rest = slice(SUBLANES, Tc)
            cv = jnp.concatenate([cv, cb + r2[:, rest, :] * w0 + r1[:, rest, :] * w1 + u[:, rest, :] * w2], axis=1)
        carry[:, :, cols] = u[:, Tc - (CONV_W - 1):Tc, :]
        return cv.reshape(G * Tc, FFN_CHUNK)

    for n in range(dff // FFN_CHUNK):
        val = conv(slice(n * FFN_CHUNK, (n + 1) * FFN_CHUNK))
        gt = conv(slice(dff + n * FFN_CHUNK, dff + (n + 1) * FFN_CHUNK))
        act_ref[:, n * FFN_CHUNK:(n + 1) * FFN_CHUNK] = (gt * _sigmoid(gt) * val).astype(BF16)
    cs_ref[...] = carry[...]
    out = h + jnp.dot(act_ref[...], wdn_ref[...], preferred_element_type=F32)
    y_ref[...] = _rms(out, lnf_ref[...]).reshape(G, Tc, D)


def _ffn(h, conv_prev, ln2, wup, cw, cb, wdn, lnf, G, Tc):
    B, T, D = h.shape
    dff = wdn.shape[0]
    assert dff % FFN_CHUNK == 0 and Tc >= CONV_W - 1 and (G == 1 or Tc == T)
    consts_a = (ln2, wup, cw, cb, wdn, lnf)
    tile = pl.BlockSpec((G, Tc, D), lambda b, c: (b, c, 0))
    cst = pl.BlockSpec((G, CONV_W - 1, 2 * dff), lambda b, c: (b, 0, 0))
    return pl.pallas_call(
        functools.partial(_ffn_kernel, dff=dff),
        grid=(B // G, T // Tc),
        in_specs=[tile, cst] + [_const_spec(a.shape) for a in consts_a],
        out_specs=[tile, cst],
        out_shape=[jax.ShapeDtypeStruct((B, T, D), F32), jax.ShapeDtypeStruct((B, CONV_W - 1, 2 * dff), F32)],
        scratch_shapes=[pltpu.VMEM((G, CONV_W - 1, 2 * dff), F32), pltpu.VMEM((G * Tc, dff), BF16)],
        compiler_params=_cparams(("parallel", "arbitrary")),
        name="convffn",
    )(h, conv_prev, *consts_a)


def _trunk(x, attend, shift_prev, wkv0, conv_prev, p, tiles):
    B, T, D = x.shape
    Bv, Tv = tiles["rows"]
    q, kt, vt, pb, sg = _inproj(x.reshape(Bv, Tv, D), p["ln1_g"], p["wq"], p["wkvt"], p["wpb"], p["wg"], tiles["tm"])
    attn = attend(q, kt, vt).reshape(Bv, Tv, WIDTH)
    pb = pb.reshape(B, T, RWKV_PROJ)
    y, g, bonus, sfin, side_out = _rwkv(pb, _shift_cols(shift_prev, _to_jh), wkv0, p["rwkv_consts"],
                                        tiles["G"], tiles["Tc"], tiles["Tw"], tiles.get("side"))
    y_transposed = y.ndim == 3
    if not y_transposed:
        y = y.reshape(Bv, Tv, WIDTH)
    h = _merge(x.reshape(Bv, Tv, D), attn, y, bonus.reshape(Bv, Tv, WIDTH), g.reshape(Bv, Tv, WIDTH), sg,
               p["lnx_g"], p["lnx_b"], p["w_br_a"], p["w_br_b"], p["w_o"], tiles["tm"], y_transposed)
    yout, conv_new = _ffn(h.reshape(B, T, D), conv_prev, p["ln2_g"], p["w_up"], p["conv_w"], p["conv_b"],
                          p["w_down"], p["lnf_g"], tiles["G"], tiles["Tc"])
    return yout, kt, vt, sfin, _shift_cols(pb[:, -1, :], _from_jh), conv_new, side_out


def kernel(x_prompt, x_sample, cache_k, cache_v, page_table, state_wkv, state_tshift, state_conv, ln1_g, w_in, w_br_a, mu, w0, w2, a0, a2, g2, k_k, k_a, r_k, lnx_g, lnx_b, w_br_b, w_o, ln2_g, w_up, conv_w, conv_b, w_down, lnf_g):
    assert w_in.shape[0] == 1, "single-layer trunk"
    BP, S, D = x_prompt.shape
    DB, TS, _ = x_sample.shape
    zrows = jnp.zeros((D_AAA_LORA, WIDTH), F32)
    wpb = _shift_cols(w_in[0, :, 3 * WIDTH:3 * WIDTH + RWKV_PROJ], _to_jh)
    p = dict(
        ln1_g=ln1_g, ln2_g=ln2_g, lnf_g=lnf_g[None, :], lnx_g=_to_jh(lnx_g), lnx_b=_to_jh(lnx_b),
        wq=w_in[0, :, 0:WIDTH].astype(BF16),
        wkvt=jnp.transpose(w_in[0, :, WIDTH:3 * WIDTH]).astype(BF16),
        wpb=wpb.astype(BF16),
        wg=w_in[0, :, 3 * WIDTH + RWKV_PROJ:].astype(BF16),
        w_br_a=w_br_a[0].astype(BF16), w_br_b=_to_jh(w_br_b[0].T).T.astype(BF16), w_o=w_o[0].astype(BF16),
        w_up=w_up[0].astype(BF16), w_down=w_down[0].astype(BF16), conv_w=conv_w[0], conv_b=conv_b,
    )
    p["rwkv_consts"] = (_shift_cols(mu, _to_jh), _to_jh(w0), _to_jh(jnp.concatenate([w2[0], zrows], axis=0)),
                        _to_jh(a0), _to_jh(jnp.concatenate([zrows, a2[0]], axis=0)), _to_jh(g2[0]),
                        _to_jh(k_k), _to_jh(k_a), _to_jh(r_k.reshape(1, WIDTH)))

    ck = jnp.transpose(cache_k[0], (0, 2, 3, 1))
    cv = jnp.transpose(cache_v[0], (0, 2, 3, 1))

    tiles_p = dict(rows=(BP, S), tm=min(512, S), G=1, Tc=min(512, S), Tw=min(16, S), side=(ck, page_table))
    yp, ktp, vtp, wkv_p, tshift_p, conv_p, km = _trunk(
        x_prompt, lambda q, kt, vt: _moba_prompt(q, kt, vt),
        jnp.zeros((BP, RWKV_PROJ), F32), jnp.zeros((BP, N_HEADS, HEAD_DIM, HEAD_DIM), F32),
        jnp.zeros((BP, CONV_W - 1, w_up.shape[2]), F32), p, tiles_p)
    if km is None:
        km = _kmean(ck, page_table)

    tiles_s = dict(rows=(1, DB * TS), tm=DB * TS, G=DB, Tc=TS, Tw=TS)
    attend_s = lambda q, kt, vt: _moba_sample(q, kt, vt, ck, cv, page_table, km)
    ys, kts, vts, wkv_s, tshift_s, conv_s, _ = _trunk(
        x_sample, attend_s, state_tshift[0], state_wkv[0], state_conv[0], p, tiles_s)

    heads_p = lambda t: jnp.transpose(t.reshape(BP, N_HEADS, HEAD_DIM, S), (0, 3, 1, 2))[None]
    heads_s = lambda t: jnp.transpose(t.reshape(N_HEADS, HEAD_DIM, DB, TS), (2, 3, 0, 1))[None]
    return (yp, ys, heads_p(ktp), heads_p(vtp), wkv_p[None], tshift_p[None], conv_p[None],
            heads_s(kts), heads_s(vts), wkv_s[None], tshift_s[None], conv_s[None])
```

```python
import functools

import jax
import jax.numpy as jnp
from jax import lax
from jax.experimental import pallas as pl
from jax.experimental.pallas import tpu as pltpu

F32 = jnp.float32
BF16 = jnp.bfloat16
HIGHEST = lax.Precision.HIGHEST
NT_DIMS = (((1,), (1,)), ((), ()))

HEAD_DIM = 64
N_HEADS = 8
WIDTH = N_HEADS * HEAD_DIM
MOBA_BLOCK = 256
MOBA_TOPK = 3
PAGE_SIZE = 128
PAGES_PER_BLOCK = MOBA_BLOCK // PAGE_SIZE
D_DECAY_LORA = 64
D_AAA_LORA = 64
D_GATE_LORA = 128
RWKV_PROJ = 3 * WIDTH + D_DECAY_LORA + D_AAA_LORA + D_GATE_LORA
CONV_W = 3
RMS_EPS = 1e-6
GN_EPS = 64e-5
NEG = -1e30
ATTN_SCALE = HEAD_DIM ** -0.5

LANES = 128
SUBLANES = 8
VMEM_LIMIT = 56 * 1024 * 1024


def _cparams(sem):
    return pltpu.CompilerParams(dimension_semantics=sem, vmem_limit_bytes=VMEM_LIMIT)


def _const_spec(shape):
    nd = len(shape)
    return pl.BlockSpec(shape, lambda *_: (0,) * nd, pipeline_mode=pl.Buffered(1))


def _sigmoid(x):
    return 1.0 / (1.0 + jnp.exp(-x))


def _rms(x, g):
    return x * lax.rsqrt(jnp.mean(x * x, axis=-1, keepdims=True) + RMS_EPS) * g


def _inproj_kernel(x_ref, g_ref, wq_ref, wkvt_ref, wpb_ref, wg_ref,
                   q_ref, kt_ref, vt_ref, pb_ref, sg_ref):
    xb = _rms(x_ref[...], g_ref[...]).astype(BF16)
    q_ref[...] = jnp.dot(xb, wq_ref[...], preferred_element_type=F32)
    kt_ref[...] = lax.dot_general(wkvt_ref[0:WIDTH, :], xb, NT_DIMS, preferred_element_type=F32)
    vt_ref[...] = lax.dot_general(wkvt_ref[WIDTH:2 * WIDTH, :], xb, NT_DIMS, preferred_element_type=F32)
    pb_ref[...] = jnp.dot(xb, wpb_ref[...], preferred_element_type=F32)
    sg_ref[...] = _sigmoid(jnp.dot(xb, wg_ref[...], preferred_element_type=F32)).astype(sg_ref.dtype)


def _inproj(x, ln_g, wq, wkvt, wpb, wg, tm):
    B, T, D = x.shape
    row = lambda w: pl.BlockSpec((None, tm, w), lambda b, t: (b, t, 0))
    col = pl.BlockSpec((None, WIDTH, tm), lambda b, t: (b, 0, t))
    return pl.pallas_call(
        _inproj_kernel,
        grid=(B, T // tm),
        in_specs=[row(D), _const_spec((1, D)), _const_spec(wq.shape), _const_spec(wkvt.shape),
                  _const_spec(wpb.shape), _const_spec(wg.shape)],
        out_specs=[row(WIDTH), col, col, row(RWKV_PROJ), row(2 * D)],
        out_shape=[jax.ShapeDtypeStruct((B, T, WIDTH), F32),
                   jax.ShapeDtypeStruct((B, WIDTH, T), F32),
                   jax.ShapeDtypeStruct((B, WIDTH, T), F32),
                   jax.ShapeDtypeStruct((B, T, RWKV_PROJ), F32),
                   jax.ShapeDtypeStruct((B, T, 2 * D), BF16)],
        compiler_params=_cparams(("parallel", "parallel")),
        name="inproj",
    )(x, ln_g, wq, wkvt, wpb, wg)


def _moba_prompt_kernel(q_ref, kt_ref, vt_ref, o_ref, kp_ref, vtb_ref, qa_ref, s_ref, p_ref, *, nblk):
    S = q_ref.shape[0]
    kt = kt_ref[...]
    vtb_ref[...] = vt_ref[...].astype(BF16)
    lane = lax.broadcasted_iota(jnp.int32, (1, LANES), 1)
    rowd = lax.broadcasted_iota(jnp.int32, (LANES, 1), 0)
    keyblk = lax.broadcasted_iota(jnp.int32, (1, S), 1) // MOBA_BLOCK
    km = jnp.zeros((LANES, LANES), F32)
    first = rowd < HEAD_DIM
    for j in range(nblk):
        colj = jnp.sum(kt[:, j * MOBA_BLOCK:(j + 1) * MOBA_BLOCK], axis=1, keepdims=True) * (1.0 / MOBA_BLOCK)
        km = jnp.where(((lane == j) & jnp.logical_not(first)) | ((lane == HEAD_DIM + j) & first), colj, km)
    qrow = lax.broadcasted_iota(jnp.int32, (MOBA_BLOCK, 1), 0)
    kcol = lax.broadcasted_iota(jnp.int32, (1, MOBA_BLOCK), 1)
    for hh in range(2):
        in_r = (rowd >= HEAD_DIM * hh) & (rowd < HEAD_DIM * (hh + 1))
        ind = jnp.where(rowd - HEAD_DIM * (1 - hh) == keyblk, 1.0, 0.0)
        kp_ref[hh] = jnp.where(in_r, kt, ind).astype(BF16)
    def select(i):
        rows = slice(i * MOBA_BLOCK, (i + 1) * MOBA_BLOCK)
        qi = q_ref[rows, :]
        jl = lane % HEAD_DIM
        upper = lane >= HEAD_DIM
        if i > 0:
            gate = jnp.dot(qi, km, precision=HIGHEST, preferred_element_type=F32)
            rank = jnp.zeros((MOBA_BLOCK, LANES), F32)
            for jp in range(i):
                cj = jnp.where(upper, gate[:, HEAD_DIM + jp:HEAD_DIM + jp + 1], gate[:, jp:jp + 1])
                beats = (cj > gate) | ((cj == gate) & (jp < jl))
                rank = rank + jnp.where(beats, 1.0, 0.0)
            keep = ((jl < i) & (rank < min(MOBA_TOPK, nblk - 1))) | (jl == i)
        else:
            keep = jl == 0
        bias = jnp.where((jl < nblk) & jnp.logical_not(keep), NEG, 0.0)
        for hh in range(2):
            in_l = upper if hh == 1 else jnp.logical_not(upper)
            qa_ref[hh, rows, :] = jnp.where(in_l, qi * ATTN_SCALE, bias).astype(BF16)
    def logits(i):
        rows = slice(i * MOBA_BLOCK, (i + 1) * MOBA_BLOCK)
        m = []
        for hh in range(2):
            qa = qa_ref[hh, rows, :]
            mx = None
            for j in range(i + 1):
                keys = slice(j * MOBA_BLOCK, (j + 1) * MOBA_BLOCK)
                s = jnp.dot(qa, kp_ref[hh, :, keys], preferred_element_type=F32)
                if j == i:
                    s = jnp.where(kcol <= qrow, s, NEG)
                s_ref[i % 2, hh, :, keys] = s
                e = jnp.maximum(s[:, 0:LANES], s[:, LANES:2 * LANES])
                mx = e if mx is None else jnp.maximum(mx, e)
            m.append(jnp.max(mx, axis=-1, keepdims=True))
        return m

    def softmax(i, m):
        l = []
        for hh in range(2):
            ls = None
            for j in range(i + 1):
                keys = slice(j * MOBA_BLOCK, (j + 1) * MOBA_BLOCK)
                p = jnp.exp(s_ref[i % 2, hh, :, keys] - m[hh])
                p_ref[i % 2, hh, :, keys] = p.astype(BF16)
                e = p[:, 0:LANES] + p[:, LANES:2 * LANES]
                ls = e if ls is None else ls + e
            l.append(jnp.sum(ls, axis=-1, keepdims=True))
        return l

    def pv(i, l):
        rows = slice(i * MOBA_BLOCK, (i + 1) * MOBA_BLOCK)
        nk = (i + 1) * MOBA_BLOCK
        outs = [lax.dot_general(p_ref[i % 2, hh, :, 0:nk], vtb_ref[:, 0:nk], NT_DIMS,
                                preferred_element_type=F32) / l[hh] for hh in range(2)]
        o_ref[rows, :] = jnp.where(lane < HEAD_DIM, outs[0], outs[1]).astype(o_ref.dtype)

    for i in range(nblk):
        select(i)
    m_next = logits(0)
    for i in range(nblk):
        m_cur = m_next
        if i + 1 < nblk:
            m_next = logits(i + 1)
        pv(i, softmax(i, m_cur))


def _moba_prompt(q, kt, vt):
    B, S, _ = q.shape
    nblk = S // MOBA_BLOCK
    qspec = pl.BlockSpec((None, S, LANES), lambda b, h: (b, 0, h))
    tspec = pl.BlockSpec((None, LANES, S), lambda b, h: (b, h, 0))
    return pl.pallas_call(
        functools.partial(_moba_prompt_kernel, nblk=nblk),
        grid=(B, WIDTH // LANES),
        in_specs=[qspec, tspec, tspec],
        out_specs=qspec,
        out_shape=jax.ShapeDtypeStruct((B, S, WIDTH), BF16),
        scratch_shapes=[pltpu.VMEM((2, LANES, S), BF16), pltpu.VMEM((LANES, S), BF16), pltpu.VMEM((2, S, LANES), BF16),
                        pltpu.VMEM((2, 2, MOBA_BLOCK, S), F32), pltpu.VMEM((2, 2, MOBA_BLOCK, S), BF16)],
        compiler_params=_cparams(("parallel", "parallel")),
        name="moba_prompt",
    )(q, kt, vt)


def _kmean_accumulate(pages, km_ref, chunk):
    @pl.when(chunk == 0)
    def _():
        km_ref[...] = jnp.zeros(km_ref.shape, F32)

    lane = lax.broadcasted_iota(jnp.int32, (1, 1, LANES), 2)
    blocks_per_step = len(pages) // PAGES_PER_BLOCK
    for jj in range(blocks_per_step):
        tot = pages[PAGES_PER_BLOCK * jj][...]
        for r in range(1, PAGES_PER_BLOCK):
            tot = tot + pages[PAGES_PER_BLOCK * jj + r][...]
        colj = jnp.sum(tot, axis=-1, keepdims=True) * (1.0 / MOBA_BLOCK)
        km_ref[...] = jnp.where(lane == chunk * blocks_per_step + jj, colj, km_ref[...])


def _kmean_specs(n_pages, pages_per_step, step_of):
    spq = n_pages // pages_per_step

    def page_spec(r):
        def index(*ids):
            step, pt = step_of(*ids[:-1]), ids[-1]
            return (pt[step // spq, (step % spq) * pages_per_step + r], 0, 0, 0)
        return pl.BlockSpec((None, N_HEADS, HEAD_DIM, PAGE_SIZE), index)

    km_spec = pl.BlockSpec((None, N_HEADS, HEAD_DIM, LANES), lambda *ids: (step_of(*ids[:-1]) // spq, 0, 0, 0))
    return [page_spec(r) for r in range(pages_per_step)], km_spec


def _kmean_kernel(pt_ref, *refs):
    del pt_ref
    _kmean_accumulate(refs[:-1], refs[-1], pl.program_id(1))


def _kmean(ck, page_table, pages_per_step=32):
    DB, n_pages = page_table.shape
    pages_per_step = min(pages_per_step, n_pages)
    assert n_pages // PAGES_PER_BLOCK <= LANES and n_pages % pages_per_step == 0
    spq = n_pages // pages_per_step
    page_specs, km_spec = _kmean_specs(n_pages, pages_per_step, lambda b, c: b * spq + c)
    return pl.pallas_call(
        _kmean_kernel,
        grid_spec=pltpu.PrefetchScalarGridSpec(
            num_scalar_prefetch=1, grid=(DB, spq), in_specs=page_specs, out_specs=km_spec),
        out_shape=jax.ShapeDtypeStruct((DB, N_HEADS, HEAD_DIM, LANES), F32),
        compiler_params=_cparams(("parallel", "arbitrary")),
        name="moba_kmean",
    )(page_table, *([ck] * pages_per_step))


def _select_kernel(q_ref, km_ref, sel_ref, *, nblk):
    lane = lax.broadcasted_iota(jnp.int32, (1, LANES), 1)
    lanef = lane.astype(F32)
    for h in range(N_HEADS):
        gate = jnp.dot(q_ref[h], km_ref[h], precision=HIGHEST, preferred_element_type=F32)
        gate = jnp.where(lane < nblk, gate, -jnp.inf)
        out = jnp.zeros(gate.shape, F32)
        for r in range(MOBA_TOPK):
            m = jnp.max(gate, axis=-1, keepdims=True)
            idx = jnp.min(jnp.where(gate == m, lanef, float(LANES)), axis=-1, keepdims=True)
            out = jnp.where(lane == r, idx, out)
            gate = jnp.where(lanef == idx, -jnp.inf, gate)
        sel_ref[h] = out.astype(jnp.int32)


def _select(qh, km, nblk):
    DB, H, T, Dh = qh.shape
    return pl.pallas_call(
        functools.partial(_select_kernel, nblk=nblk),
        grid=(DB,),
        in_specs=[pl.BlockSpec((None, H, T, Dh), lambda b: (b, 0, 0, 0)),
                  pl.BlockSpec((None, H, Dh, LANES), lambda b: (b, 0, 0, 0))],
        out_specs=pl.BlockSpec((None, H, T, LANES), lambda b: (b, 0, 0, 0)),
        out_shape=jax.ShapeDtypeStruct((DB, H, T, LANES), jnp.int32),
        compiler_params=_cparams(("parallel",)),
        name="moba_select",
    )(qh, km)


SAMPLE_HEADS_PER_STEP = 1


def _sample_attn_kernel(sel_ref, pt_ref, qt_ref, knt_ref, vnt_ref, ck_hbm, cv_hbm, o_ref, kbuf, vbuf, sems,
                        *, T, n_sel, n_heads):
    b, hg = pl.program_id(0), pl.program_id(1)
    hp = SAMPLE_HEADS_PER_STEP
    ng = n_heads // hp
    step = b * ng + hg
    nstep = pl.num_programs(0) * ng

    def copies(bb, gg, slot, lookup):
        out = []
        for e in range(hp):
            hh = gg * hp + e
            for t in range(T):
                for s in range(n_sel):
                    blk = sel_ref[((bb * n_heads + hh) * T + t) * n_sel + s] if lookup else 0
                    for r in range(PAGES_PER_BLOCK):
                        page = pt_ref[bb, blk * PAGES_PER_BLOCK + r] if lookup else 0
                        dst = pl.ds((s * PAGES_PER_BLOCK + r) * PAGE_SIZE, PAGE_SIZE)
                        out.append(pltpu.make_async_copy(ck_hbm.at[page, hh], kbuf.at[slot, e, t, :, dst],
                                                         sems.at[slot]))
                        out.append(pltpu.make_async_copy(cv_hbm.at[page, hh], vbuf.at[slot, e, t, :, dst],
                                                         sems.at[slot]))
        return out

    @pl.when(step == 0)
    def _():
        for cp in copies(b, hg, 0, True):
            cp.start()

    @pl.when(step + 1 < nstep)
    def _():
        nxt = step + 1
        for cp in copies(nxt // ng, nxt % ng, nxt % 2, True):
            cp.start()

    slot = step % 2
    for cp in copies(0, 0, slot, False):
        cp.wait()

    lane_t = lax.broadcasted_iota(jnp.int32, (1, T), 1)
    row_t = lax.broadcasted_iota(jnp.int32, (T, 1), 0)
    lane = lax.broadcasted_iota(jnp.int32, (1, LANES), 1)
    for e in range(hp):
        qt = qt_ref[e]
        knt = knt_ref[e]
        vnt = vnt_ref[e]
        s_past, s_new = [], []
        for t in range(T):
            qc = qt[:, t:t + 1]
            s_past.append(jnp.sum(kbuf[slot, e, t] * qc, axis=0, keepdims=True))
            s_new.append(jnp.sum(knt * qc, axis=0, keepdims=True))
        s_past = jnp.concatenate(s_past, axis=0) * ATTN_SCALE
        s_new = jnp.concatenate(s_new, axis=0) * ATTN_SCALE
        s_new = jnp.where(lane_t <= row_t, s_new, NEG)
        m = jnp.maximum(jnp.max(s_past, axis=-1, keepdims=True), jnp.max(s_new, axis=-1, keepdims=True))
        p_past = jnp.exp(s_past - m)
        p_new = jnp.exp(s_new - m)
        l = jnp.sum(p_past, axis=-1, keepdims=True) + jnp.sum(p_new, axis=-1, keepdims=True)
        out = jnp.zeros((HEAD_DIM, LANES), F32)
        for t in range(T):
            o = (jnp.sum(vbuf[slot, e, t] * p_past[t:t + 1, :], axis=-1, keepdims=True)
                 + jnp.sum(vnt * p_new[t:t + 1, :], axis=-1, keepdims=True)) / l[t:t + 1, :]
            out = jnp.where(lane == t, o, out)
        o_ref[e] = out


def _sample_attn(sel_flat, page_table, qt, knt, vnt, ck, cv, n_sel):
    DB, H, Dh, T = qt.shape
    hp = SAMPLE_HEADS_PER_STEP
    assert H % hp == 0
    small = pl.BlockSpec((None, hp, Dh, T), lambda b, g, *_: (b, g, 0, 0))
    nkeys = n_sel * MOBA_BLOCK
    return pl.pallas_call(
        functools.partial(_sample_attn_kernel, T=T, n_sel=n_sel, n_heads=H),
        grid_spec=pltpu.PrefetchScalarGridSpec(
            num_scalar_prefetch=2,
            grid=(DB, H // hp),
            in_specs=[small, small, small, pl.BlockSpec(memory_space=pl.ANY), pl.BlockSpec(memory_space=pl.ANY)],
            out_specs=pl.BlockSpec((None, hp, Dh, LANES), lambda b, g, *_: (b, g, 0, 0)),
            scratch_shapes=[pltpu.VMEM((2, hp, T, Dh, nkeys), F32), pltpu.VMEM((2, hp, T, Dh, nkeys), F32),
                            pltpu.SemaphoreType.DMA((2,))],
        ),
        out_shape=jax.ShapeDtypeStruct((DB, H, Dh, LANES), F32),
        compiler_params=_cparams(("arbitrary", "arbitrary")),
        name="moba_sample",
    )(sel_flat, page_table, qt, knt, vnt, ck, cv)


def _moba_sample(q, kt, vt, ck, cv, page_table, km):
    DB, n_pages = page_table.shape
    T = q.shape[1] // DB
    nblk = n_pages // PAGES_PER_BLOCK
    n_sel = min(MOBA_TOPK, nblk)
    assert n_pages % PAGES_PER_BLOCK == 0 and n_sel == MOBA_TOPK
    qh = jnp.transpose(q.reshape(DB, T, N_HEADS, HEAD_DIM), (0, 2, 1, 3))
    qt = jnp.transpose(qh, (0, 1, 3, 2))
    knt = jnp.transpose(kt.reshape(N_HEADS, HEAD_DIM, DB, T), (2, 0, 1, 3))
    vnt = jnp.transpose(vt.reshape(N_HEADS, HEAD_DIM, DB, T), (2, 0, 1, 3))
    sel = _select(qh, km, nblk)[..., :n_sel].reshape(-1)
    o = _sample_attn(sel, page_table, qt, knt, vnt, ck, cv, n_sel)
    return jnp.transpose(o[..., :T], (0, 3, 1, 2)).reshape(DB * T, WIDTH)


def _to_jh(x):
    lead = x.shape[:-1]
    return jnp.swapaxes(x.reshape(*lead, N_HEADS, HEAD_DIM), -1, -2).reshape(*lead, WIDTH)


def _from_jh(x):
    lead = x.shape[:-1]
    return jnp.swapaxes(x.reshape(*lead, HEAD_DIM, N_HEADS), -1, -2).reshape(*lead, WIDTH)


def _shift_cols(x, reorder):
    parts = [reorder(x[..., i * WIDTH:(i + 1) * WIDTH]) for i in range(3)]
    return jnp.concatenate(parts + [x[..., 3 * WIDTH:]], axis=-1)


def _head_sum(x):
    z = x[:, 0:LANES]
    for p in range(1, x.shape[1] // LANES):
        z = z + x[:, p * LANES:(p + 1) * LANES]
    shift = N_HEADS
    while shift < LANES:
        z = z + pltpu.roll(z, shift, axis=1)
        shift *= 2
    return jnp.concatenate([z] * (x.shape[1] // LANES), axis=1)


def _rwkv_prep_kernel(pb_ref, prev_ref, mu_ref, w0_ref, w2p_ref, a0_ref, a2p_ref, g2_ref, kk_ref, ka_ref,
                      rk_ref, r_ref, w_ref, k_ref, v_ref, a_ref, b_ref, g_ref, bonus_ref, carry, *, transposed):
    c = pl.program_id(1)

    @pl.when(c == 0)
    def _():
        carry[...] = prev_ref[...]

    pb3 = pb_ref[...]
    G, Tc, P = pb3.shape
    tpos = lax.broadcasted_iota(jnp.int32, (1, Tc, 1), 1)
    prev3 = jnp.where(tpos == 0, carry[...], pltpu.roll(pb3, 1, axis=1))
    carry[...] = pb3[:, Tc - 1:Tc, :]
    pb = pb3.reshape(G * Tc, P)
    xm = pb + (prev3.reshape(G * Tc, P) - pb) * mu_ref[...]
    r = xm[:, 0:WIDTH]
    k = xm[:, WIDTH:2 * WIDTH]
    v = xm[:, 2 * WIDTH:3 * WIDTH]
    wa = xm[:, 3 * WIDTH:3 * WIDTH + D_DECAY_LORA + D_AAA_LORA]
    gd = xm[:, 3 * WIDTH + D_DECAY_LORA + D_AAA_LORA:]
    z = w0_ref[...] + jnp.dot(jnp.tanh(wa), w2p_ref[...], precision=HIGHEST, preferred_element_type=F32)
    softplus_negz = jnp.maximum(-z, 0.0) + jnp.log(1.0 + jnp.exp(-jnp.abs(z)))
    w = jnp.exp(-jnp.exp(-softplus_negz - 0.5))
    lr = _sigmoid(a0_ref[...] + jnp.dot(wa, a2p_ref[...], precision=HIGHEST, preferred_element_type=F32))
    g_ref[...] = jnp.dot(_sigmoid(gd), g2_ref[...], precision=HIGHEST, preferred_element_type=F32)
    kk = k * kk_ref[...]
    kk = kk / jnp.maximum(jnp.sqrt(_head_sum(kk * kk)), 1e-12)
    k2 = k * (1.0 + (lr - 1.0) * ka_ref[...])
    bonus_ref[...] = _head_sum(r * k2 * rk_ref[...]) * v
    outs = ((r_ref, r), (w_ref, w), (k_ref, k2), (v_ref, v), (a_ref, -kk), (b_ref, kk * lr))
    for ref, val in outs:
        ref[...] = val.T if transposed else val


def _rwkv_prep(pb, prev_row, consts, G, Tc, transposed):
    B, T, P = pb.shape
    nt = T // Tc
    assert G == 1 or nt == 1
    nat = jax.ShapeDtypeStruct((B * T, WIDTH), F32)
    nat_spec = pl.BlockSpec((G * Tc, WIDTH), lambda b, c: (b * nt + c, 0))
    if transposed:
        assert G == 1
        vec, vec_spec = jax.ShapeDtypeStruct((B, WIDTH, T), F32), pl.BlockSpec((None, WIDTH, Tc), lambda b, c: (b, 0, c))
    else:
        vec, vec_spec = nat, nat_spec
    return pl.pallas_call(
        functools.partial(_rwkv_prep_kernel, transposed=transposed),
        grid=(B // G, nt),
        in_specs=[pl.BlockSpec((G, Tc, P), lambda b, c: (b, c, 0)),
                  pl.BlockSpec((G, 1, P), lambda b, c: (b, 0, 0))] + [_const_spec(a.shape) for a in consts],
        out_specs=[vec_spec] * 6 + [nat_spec] * 2,
        out_shape=[vec] * 6 + [nat] * 2,
        scratch_shapes=[pltpu.VMEM((G, 1, P), F32)],
        compiler_params=_cparams(("parallel", "arbitrary")),
        name="rwkv_prep",
    )(pb, prev_row, *consts)


LANE_CHUNK = 128


def _to_lanes_kernel(x_ref, o_ref, *, tw):
    nb = x_ref.shape[0]
    for j in range(HEAD_DIM):
        m = x_ref[:, j * N_HEADS:(j + 1) * N_HEADS, :].reshape(nb * N_HEADS, LANE_CHUNK)
        mt = m.T
        for q in range(LANE_CHUNK // tw):
            o_ref[q, j] = mt[q * tw:(q + 1) * tw, :]


def _to_lanes(xt, tw):
    B, W, T = xt.shape
    L = B * N_HEADS
    assert L <= LANES and T % LANE_CHUNK == 0 and LANE_CHUNK % tw == 0
    return pl.pallas_call(
        functools.partial(_to_lanes_kernel, tw=tw),
        grid=(T // LANE_CHUNK,),
        in_specs=[pl.BlockSpec((B, W, LANE_CHUNK), lambda c: (0, 0, c))],
        out_specs=pl.BlockSpec((LANE_CHUNK // tw, HEAD_DIM, tw, L), lambda c: (c, 0, 0, 0)),
        out_shape=jax.ShapeDtypeStruct((T // tw, HEAD_DIM, tw, L), F32),
        compiler_params=_cparams(("parallel",)),
        name="to_lanes",
    )(xt)


def _from_lanes_kernel(y_ref, o_ref):
    nb = o_ref.shape[0]
    for i in range(HEAD_DIM):
        m = y_ref[pl.ds(i, LANE_CHUNK, stride=HEAD_DIM), :]
        o_ref[:, i * N_HEADS:(i + 1) * N_HEADS, :] = m.T.reshape(nb, N_HEADS, LANE_CHUNK)


def _from_lanes(y2d, B):
    TN, L = y2d.shape
    T = TN // HEAD_DIM
    assert L == B * N_HEADS <= LANES and T % LANE_CHUNK == 0
    return pl.pallas_call(
        _from_lanes_kernel,
        grid=(T // LANE_CHUNK,),
        in_specs=[pl.BlockSpec((LANE_CHUNK * HEAD_DIM, L), lambda c: (c, 0))],
        out_specs=pl.BlockSpec((B, WIDTH, LANE_CHUNK), lambda c: (0, 0, c)),
        out_shape=jax.ShapeDtypeStruct((B, WIDTH, T), F32),
        compiler_params=_cparams(("parallel",)),
        name="from_lanes",
    )(y2d)


WKV_UNROLL = 32


def _wkv_kernel(*refs, side_pages, side_steps, side_spq):
    if side_pages:
        pt_ref, refs = refs[0], refs[1:]
        r_ref, w_ref, k_ref, v_ref, a_ref, b_ref, s0_ref, ck_hbm, y_ref, sfin_ref, km_ref, S, pbuf, sems = refs
    else:
        r_ref, w_ref, k_ref, v_ref, a_ref, b_ref, s0_ref, y_ref, sfin_ref, S = refs
    c = pl.program_id(1)
    N, Tw = r_ref.shape[0], r_ref.shape[1]
    trips = N // WKV_UNROLL

    @pl.when(c == 0)
    def _():
        S[...] = s0_ref[...]

    if side_pages:
        def page_copies(s, lookup):
            seq, first = s // side_spq, (s % side_spq) * side_pages
            return [pltpu.make_async_copy(ck_hbm.at[pt_ref[seq, first + r] if lookup else 0],
                                          pbuf.at[s % 2, r], sems.at[s % 2]) for r in range(side_pages)]

        @pl.when(c == 0)
        def _():
            for cp in page_copies(c, True):
                cp.start()

        @pl.when(c + 1 < side_steps)
        def _():
            for cp in page_copies(c + 1, True):
                cp.start()

        @pl.when(c < side_steps)
        def _():
            for cp in page_copies(c, False):
                cp.wait()
            _kmean_accumulate([pbuf.at[c % 2, r] for r in range(side_pages)], km_ref, c % side_spq)

    def token(t, carry):
        row = lambda ref, j: ref[j, pl.ds(t, 1), :]

        def sa_body(jb, sa):
            for jj in range(WKV_UNROLL):
                j = jb * WKV_UNROLL + jj
                sa = sa + S[j] * row(a_ref, j)
            return sa

        sa = lax.fori_loop(0, trips, sa_body, jnp.zeros(S.shape[1:], F32))
        vt = v_ref[pl.ds(t, N, stride=Tw), :]

        def up_body(jb, y):
            for jj in range(WKV_UNROLL):
                j = jb * WKV_UNROLL + jj
                s = S[j] * row(w_ref, j) + sa * row(b_ref, j) + vt * row(k_ref, j)
                S[j] = s
                y = y + s * row(r_ref, j)
            return y

        y_ref[t] = lax.fori_loop(0, N // WKV_UNROLL, up_body, jnp.zeros(S.shape[1:], F32))
        return carry

    lax.fori_loop(0, Tw, token, 0)

    @pl.when(c == pl.num_programs(1) - 1)
    def _():
        sfin_ref[...] = S[...]


def _side_pages_per_step(page_table, n_steps, tokens_per_step):
    DB, n_pages = page_table.shape
    for pps in range(PAGES_PER_BLOCK, min(n_pages, tokens_per_step * PAGES_PER_BLOCK) + 1, PAGES_PER_BLOCK):
        if n_pages % pps == 0 and 2 <= DB * (n_pages // pps) <= n_steps:
            return pps
    return 0


def _wkv(r, w, k, v, a, b, s0, side=None):
    nc, N, Tw, L = r.shape
    lb = min(LANES, L)
    vec = pl.BlockSpec((None, N, Tw, lb), lambda g, c, *_: (c, 0, 0, g))
    st = pl.BlockSpec((N, N, lb), lambda g, c, *_: (0, 0, g))
    in_specs = [vec, vec, vec, pl.BlockSpec((N * Tw, lb), lambda g, c, *_: (c, g)), vec, vec, st]
    out_specs = [pl.BlockSpec((Tw, N, lb), lambda g, c, *_: (c, 0, g)), st]
    out_shape = [jax.ShapeDtypeStruct((nc * Tw, N, L), F32), jax.ShapeDtypeStruct((N, N, L), F32)]
    operands = [r, w, k, v.reshape(nc * N * Tw, L), a, b, s0]
    scratch = [pltpu.VMEM((N, N, lb), F32)]
    prefetch, pps, side_steps, spq = [], 0, 0, 1
    if side is not None:
        ck, page_table = side
        assert L == lb
        pps = _side_pages_per_step(page_table, nc, Tw)
        spq = page_table.shape[1] // pps
        side_steps = page_table.shape[0] * spq
        in_specs.append(pl.BlockSpec(memory_space=pl.ANY))
        out_specs.append(pl.BlockSpec((None, N_HEADS, HEAD_DIM, LANES),
                                      lambda g, c, pt: (jnp.minimum(c, side_steps - 1) // spq, 0, 0, 0)))
        out_shape.append(jax.ShapeDtypeStruct((page_table.shape[0], N_HEADS, HEAD_DIM, LANES), F32))
        operands.append(ck)
        prefetch = [page_table]
        scratch += [pltpu.VMEM((2, pps) + ck.shape[1:], F32), pltpu.SemaphoreType.DMA((2,))]
    return pl.pallas_call(
        functools.partial(_wkv_kernel, side_pages=pps, side_steps=side_steps, side_spq=spq),
        grid_spec=pltpu.PrefetchScalarGridSpec(
            num_scalar_prefetch=len(prefetch), grid=(L // lb, nc), in_specs=in_specs, out_specs=out_specs,
            scratch_shapes=scratch),
        out_shape=out_shape,
        compiler_params=_cparams(("parallel", "arbitrary")),
        name="rwkv_wkv",
    )(*prefetch, *operands)


def _rwkv(pb, prev_row, wkv0, consts, G, Tc, Tw, side=None):
    B, T, _ = pb.shape
    L = B * N_HEADS
    big = T % LANE_CHUNK == 0
    if side is not None and not (big and _side_pages_per_step(side[1], T // Tw, Tw)):
        side = None
    r, w, k, v, a, b, g, bonus = _rwkv_prep(pb, prev_row[:, None, :], consts, G, Tc, big)
    if big:
        coef = [_to_lanes(x, Tw) for x in (r, w, k, v, a, b)]
    else:
        assert Tw == T
        coef = jnp.stack([r, w, k, v, a, b]).reshape(6, B, T, HEAD_DIM, N_HEADS)
        coef = jnp.transpose(coef, (0, 3, 2, 1, 4)).reshape(6, 1, HEAD_DIM, T, L)
    s0 = jnp.transpose(wkv0, (3, 2, 0, 1)).reshape(HEAD_DIM, HEAD_DIM, L)
    y, sfin, *side_out = _wkv(*coef, s0, side)
    if big:
        y = _from_lanes(y.reshape(T * HEAD_DIM, L), B)
    else:
        y = jnp.transpose(y.reshape(T, HEAD_DIM, B, N_HEADS), (2, 0, 1, 3)).reshape(B * T, WIDTH)
    sfin = jnp.transpose(sfin.reshape(HEAD_DIM, HEAD_DIM, B, N_HEADS), (2, 3, 1, 0))
    return y, g, bonus, sfin, (side_out[0] if side_out else None)


def _merge_kernel(x_ref, at_ref, y_ref, bonus_ref, g_ref, sg_ref, lng_ref, lnb_ref,
                  wa_ref, wb_ref, wo_ref, h_ref, *, y_transposed):
    D = x_ref.shape[1]
    y = y_ref[...].T if y_transposed else y_ref[...]
    d = y - _head_sum(y) * (1.0 / HEAD_DIM)
    var = _head_sum(d * d) * (1.0 / HEAD_DIM)
    yn = d * lax.rsqrt(var + GN_EPS) * lng_ref[...] + lnb_ref[...]
    ob = (yn + bonus_ref[...]) * g_ref[...]
    y_a = jnp.dot(at_ref[...].astype(BF16), wa_ref[...], preferred_element_type=F32)
    y_b = jnp.dot(ob.astype(BF16), wb_ref[...], preferred_element_type=F32)
    mix = sg_ref[:, 0:D].astype(F32) * y_a + sg_ref[:, D:2 * D].astype(F32) * y_b
    h_ref[...] = x_ref[...] + jnp.dot(mix.astype(BF16), wo_ref[...], preferred_element_type=F32)


def _merge(x, attn, y, bonus, g, sg, lng, lnb, wa, wb, wo, tm, y_transposed):
    B, T, D = x.shape
    row = lambda w: pl.BlockSpec((None, tm, w), lambda b, t: (b, t, 0))
    yspec = pl.BlockSpec((None, WIDTH, tm), lambda b, t: (b, 0, t)) if y_transposed else row(WIDTH)
    consts = (lng, lnb, wa, wb, wo)
    return pl.pallas_call(
        functools.partial(_merge_kernel, y_transposed=y_transposed),
        grid=(B, T // tm),
        in_specs=[row(D), row(WIDTH), yspec, row(WIDTH), row(WIDTH), row(2 * D)]
                 + [_const_spec(a.shape) for a in consts],
        out_specs=row(D),
        out_shape=jax.ShapeDtypeStruct((B, T, D), F32),
        compiler_params=_cparams(("parallel", "parallel")),
        name="merge",
    )(x, attn, y, bonus, g, sg, *consts)


FFN_CHUNK = 256


def _ffn_kernel(h_ref, cp_ref, ln2_ref, wup_ref, cw_ref, cb_ref, wdn_ref, lnf_ref, y_ref, cs_ref, carry, act_ref,
                *, dff):
    c = pl.program_id(1)

    @pl.when(c == 0)
    def _():
        carry[...] = cp_ref[...]

    h3 = h_ref[...]
    G, Tc, D = h3.shape
    h = h3.reshape(G * Tc, D)
    xb = _rms(h, ln2_ref[...]).astype(BF16)
    thead = lax.broadcasted_iota(jnp.int32, (1, SUBLANES, 1), 1)

    def conv(cols):
        u = jnp.dot(xb, wup_ref[:, cols], preferred_element_type=F32).reshape(G, Tc, FFN_CHUNK)
        w0, w1, w2, cb = cw_ref[0:1, cols], cw_ref[1:2, cols], cw_ref[2:3, cols], cb_ref[:, cols]
        r1 = pltpu.roll(u, 1, axis=1)
        r2 = pltpu.roll(u, 2, axis=1)
        p0 = carry[:, 0:1, cols]
        p1 = carry[:, 1:2, cols]
        h1 = jnp.where(thead == 0, p1, r1[:, 0:SUBLANES, :])
        h2 = jnp.where(thead == 0, p0, jnp.where(thead == 1, p1, r2[:, 0:SUBLANES, :]))
        cv = cb + h2 * w0 + h1 * w1 + u[:, 0:SUBLANES, :] * w2
        if Tc > SUBLANES:
            rest = slice(SUBLANES, Tc)
            cv = jnp.concatenate([cv, cb + r2[:, rest, :] * w0 + r1[:, rest, :] * w1 + u[:, rest, :] * w2], axis=1)
        carry[:, :, cols] = u[:, Tc - (CONV_W - 1):Tc, :]
        return cv.reshape(G * Tc, FFN_CHUNK)

    for n in range(dff // FFN_CHUNK):
        val = conv(slice(n * FFN_CHUNK, (n + 1) * FFN_CHUNK))
        gt = conv(slice(dff + n * FFN_CHUNK, dff + (n + 1) * FFN_CHUNK))
        act_ref[:, n * FFN_CHUNK:(n + 1) * FFN_CHUNK] = (gt * _sigmoid(gt) * val).astype(BF16)
    cs_ref[...] = carry[...]
    out = h + jnp.dot(act_ref[...], wdn_ref[...], preferred_element_type=F32)
    y_ref[...] = _rms(out, lnf_ref[...]).reshape(G, Tc, D)


def _ffn(h, conv_prev, ln2, wup, cw, cb, wdn, lnf, G, Tc):
    B, T, D = h.shape
    dff = wdn.shape[0]
    assert dff % FFN_CHUNK == 0 and Tc >= CONV_W - 1 and (G == 1 or Tc == T)
    consts_a = (ln2, wup, cw, cb, wdn, lnf)
    tile = pl.BlockSpec((G, Tc, D), lambda b, c: (b, c, 0))
    cst = pl.BlockSpec((G, CONV_W - 1, 2 * dff), lambda b, c: (b, 0, 0))
    return pl.pallas_call(
        functools.partial(_ffn_kernel, dff=dff),
        grid=(B // G, T // Tc),
        in_specs=[tile, cst] + [_const_spec(a.shape) for a in consts_a],
        out_specs=[tile, cst],
        out_shape=[jax.ShapeDtypeStruct((B, T, D), F32), jax.ShapeDtypeStruct((B, CONV_W - 1, 2 * dff), F32)],
        scratch_shapes=[pltpu.VMEM((G, CONV_W - 1, 2 * dff), F32), pltpu.VMEM((G * Tc, dff), BF16)],
        compiler_params=_cparams(("parallel", "arbitrary")),
        name="convffn",
    )(h, conv_prev, *consts_a)


def _trunk(x, attend, shift_prev, wkv0, conv_prev, p, tiles):
    B, T, D = x.shape
    Bv, Tv = tiles["rows"]
    q, kt, vt, pb, sg = _inproj(x.reshape(Bv, Tv, D), p["ln1_g"], p["wq"], p["wkvt"], p["wpb"], p["wg"], tiles["tm"])
    attn = attend(q, kt, vt).reshape(Bv, Tv, WIDTH)
    pb = pb.reshape(B, T, RWKV_PROJ)
    y, g, bonus, sfin, side_out = _rwkv(pb, _shift_cols(shift_prev, _to_jh), wkv0, p["rwkv_consts"],
                                        tiles["G"], tiles["Tc"], tiles["Tw"], tiles.get("side"))
    y_transposed = y.ndim == 3
    if not y_transposed:
        y = y.reshape(Bv, Tv, WIDTH)
    h = _merge(x.reshape(Bv, Tv, D), attn, y, bonus.reshape(Bv, Tv, WIDTH), g.reshape(Bv, Tv, WIDTH), sg,
               p["lnx_g"], p["lnx_b"], p["w_br_a"], p["w_br_b"], p["w_o"], tiles["tm"], y_transposed)
    yout, conv_new = _ffn(h.reshape(B, T, D), conv_prev, p["ln2_g"], p["w_up"], p["conv_w"], p["conv_b"],
                          p["w_down"], p["lnf_g"], tiles["G"], tiles["Tc"])
    return yout, kt, vt, sfin, _shift_cols(pb[:, -1, :], _from_jh), conv_new, side_out


def kernel(x_prompt, x_sample, cache_k, cache_v, page_table, state_wkv, state_tshift, state_conv, ln1_g, w_in, w_br_a, mu, w0, w2, a0, a2, g2, k_k, k_a, r_k, lnx_g, lnx_b, w_br_b, w_o, ln2_g, w_up, conv_w, conv_b, w_down, lnf_g):
    assert w_in.shape[0] == 1, "single-layer trunk"
    BP, S, D = x_prompt.shape
    DB, TS, _ = x_sample.shape
    zrows = jnp.zeros((D_AAA_LORA, WIDTH), F32)
    wpb = _shift_cols(w_in[0, :, 3 * WIDTH:3 * WIDTH + RWKV_PROJ], _to_jh)
    p = dict(
        ln1_g=ln1_g, ln2_g=ln2_g, lnf_g=lnf_g[None, :], lnx_g=_to_jh(lnx_g), lnx_b=_to_jh(lnx_b),
        wq=w_in[0, :, 0:WIDTH].astype(BF16),
        wkvt=jnp.transpose(w_in[0, :, WIDTH:3 * WIDTH]).astype(BF16),
        wpb=wpb.astype(BF16),
        wg=w_in[0, :, 3 * WIDTH + RWKV_PROJ:].astype(BF16),
        w_br_a=w_br_a[0].astype(BF16), w_br_b=_to_jh(w_br_b[0].T).T.astype(BF16), w_o=w_o[0].astype(BF16),
        w_up=w_up[0].astype(BF16), w_down=w_down[0].astype(BF16), conv_w=conv_w[0], conv_b=conv_b,
    )
    p["rwkv_consts"] = (_shift_cols(mu, _to_jh), _to_jh(w0), _to_jh(jnp.concatenate([w2[0], zrows], axis=0)),
                        _to_jh(a0), _to_jh(jnp.concatenate([zrows, a2[0]], axis=0)), _to_jh(g2[0]),
                        _to_jh(k_k), _to_jh(k_a), _to_jh(r_k.reshape(1, WIDTH)))

    ck = jnp.transpose(cache_k[0], (0, 2, 3, 1))
    cv = jnp.transpose(cache_v[0], (0, 2, 3, 1))

    tiles_p = dict(rows=(BP, S), tm=min(512, S), G=1, Tc=min(512, S), Tw=min(16, S), side=(ck, page_table))
    yp, ktp, vtp, wkv_p, tshift_p, conv_p, km = _trunk(
        x_prompt, lambda q, kt, vt: _moba_prompt(q, kt, vt),
        jnp.zeros((BP, RWKV_PROJ), F32), jnp.zeros((BP, N_HEADS, HEAD_DIM, HEAD_DIM), F32),
        jnp.zeros((BP, CONV_W - 1, w_up.shape[2]), F32), p, tiles_p)
    if km is None:
        km = _kmean(ck, page_table)

    tiles_s = dict(rows=(1, DB * TS), tm=DB * TS, G=DB, Tc=TS, Tw=TS)
    attend_s = lambda q, kt, vt: _moba_sample(q, kt, vt, ck, cv, page_table, km)
    ys, kts, vts, wkv_s, tshift_s, conv_s, _ = _trunk(
        x_sample, attend_s, state_tshift[0], state_wkv[0], state_conv[0], p, tiles_s)

    heads_p = lambda t: jnp.transpose(t.reshape(BP, N_HEADS, HEAD_DIM, S), (0, 3, 1, 2))[None]
    heads_s = lambda t: jnp.transpose(t.reshape(N_HEADS, HEAD_DIM, DB, TS), (2, 3, 0, 1))[None]
    return (yp, ys, heads_p(ktp), heads_p(vtp), wkv_p[None], tshift_p[None], conv_p[None],
            heads_s(kts), heads_s(vts), wkv_s[None], tshift_s[None], conv_s[None])
```
